```python
import jax, jax.numpy as jnp
from jax import lax
import numpy as np

D_MODEL = 1024
BATCH = 32
SEQ = 2048
DEPTH = 1

HEAD_DIM = 64
DIL_CONFIGS = ((128, 1), (512, 4), (2048, 16))
N_DIL_GROUPS = 3
DIL_HEADS = 4
DIL_WIDTH = N_DIL_GROUPS * DIL_HEADS * HEAD_DIM
DIL_OUT_WIDTH = DIL_HEADS * HEAD_DIM
GQA_Q_HEADS = 16
GQA_KV_HEADS = 4
GQA_REP = GQA_Q_HEADS // GQA_KV_HEADS
GQA_Q_WIDTH = GQA_Q_HEADS * HEAD_DIM
GQA_KV_WIDTH = GQA_KV_HEADS * HEAD_DIM
N_BRANCHES = 2
IN_WIDTH = 3 * DIL_WIDTH + GQA_Q_WIDTH + 2 * GQA_KV_WIDTH + N_BRANCHES * D_MODEL
QBLK = 128
GRID_W = 64
ROPE_THETA = 10000.0
N_EXPERTS = 256
TOP_K = 8
N_EXPERT_GROUPS = 8
TOPK_GROUPS = 4
EXPERT_FF = 256
SHARED_FF = 256
ROUTED_SCALE = 2.5
MOE_BLOCK = 128
NORM_EPS = 1e-6

kernel_name = "hybrid_dilated_gqa_moe_adaln_block"


def rms_norm(x, g):
    xf = x.astype(jnp.float32)
    y = xf * lax.rsqrt(jnp.mean(xf * xf, axis=-1, keepdims=True) + NORM_EPS)
    return (y * g.astype(jnp.float32)).astype(x.dtype)


def rope_cos_sin(pos, dim):
    inv = ROPE_THETA ** (-jnp.arange(0, dim, 2, dtype=jnp.float32) / dim)
    ang = pos.astype(jnp.float32)[:, None] * inv[None, :]
    ang = jnp.concatenate([ang, ang], axis=-1)
    return jnp.cos(ang), jnp.sin(ang)


def apply_rope(x, cos, sin):
    xf = x.astype(jnp.float32)
    x1, x2 = jnp.split(xf, 2, axis=-1)
    rot = jnp.concatenate([-x2, x1], axis=-1)
    extra = x.ndim - 3
    shp = (1, cos.shape[0]) + (1,) * extra + (cos.shape[1],)
    return (xf * cos.reshape(shp) + rot * sin.reshape(shp)).astype(x.dtype)


def dilated_window_attention(q, k, v, dil, n):
    B, S, H, hd = q.shape
    L = S // dil
    nb = -(-L // n)
    Lp = nb * n

    def to_res(a):
        a = a.reshape(B, L, dil, H, hd).transpose(0, 2, 1, 3, 4)
        return jnp.pad(a, ((0, 0), (0, 0), (0, Lp - L), (0, 0), (0, 0)))

    def band(a):
        ap = jnp.pad(a, ((0, 0), (0, 0), (n, n), (0, 0), (0, 0)))
        parts = [ap[:, :, i * n: i * n + Lp].reshape(B, dil, nb, n, H, hd) for i in range(3)]
        return jnp.concatenate(parts, axis=3)

    qb = to_res(q).reshape(B, dil, nb, n, H, hd)
    kb = band(to_res(k))
    vb = band(to_res(v))
    s = jnp.einsum('brnqhd,brnkhd->brnhqk', qb, kb).astype(jnp.float32) * (hd ** -0.5)
    blk = jnp.arange(nb)[:, None]
    q_pos = blk * n + jnp.arange(n)[None, :]
    k_pos = blk * n - n + jnp.arange(3 * n)[None, :]
    diff = k_pos[:, None, :] - q_pos[:, :, None]
    valid = (jnp.abs(diff) <= n) & (k_pos[:, None, :] >= 0) & (k_pos[:, None, :] < L)
    s = jnp.where(valid[None, None, :, None, :, :], s, -jnp.inf)
    lse = jax.nn.logsumexp(s, axis=-1)
    p = jnp.exp(s - lse[..., None])
    o = jnp.einsum('brnhqk,brnkhd->brnqhd', p.astype(v.dtype), vb)
    o = o.reshape(B, dil, Lp, H, hd)[:, :, :L].transpose(0, 2, 1, 3, 4).reshape(B, S, H, hd)
    lse = lse.transpose(0, 1, 2, 4, 3).reshape(B, dil, Lp, H)[:, :, :L]
    lse = lse.transpose(0, 2, 1, 3).reshape(B, S, H)
    return o, lse


def dilated_mixture(q, k, v):
    outs, lses = [], []
    for g, (window, dil) in enumerate(DIL_CONFIGS):
        o, lse = dilated_window_attention(q[:, :, g], k[:, :, g], v[:, :, g], dil, window // (2 * dil))
        outs.append(o)
        lses.append(lse)
    w = jax.nn.softmax(jnp.stack(lses, axis=0), axis=0)
    o = jnp.sum(w[..., None] * jnp.stack(outs, axis=0).astype(jnp.float32), axis=0)
    B, S = q.shape[:2]
    return o.astype(q.dtype).reshape(B, S, DIL_OUT_WIDTH)


def gqa_blocked(q, k, v):
    B, S, _, hd = q.shape
    nqb = S // QBLK
    qb = q.reshape(B, nqb, QBLK, GQA_KV_HEADS, GQA_REP, hd).transpose(1, 0, 2, 3, 4, 5)

    def one_block(qblk):
        s = jnp.einsum('bqgrd,bkgd->bgrqk', qblk, k).astype(jnp.float32) * (hd ** -0.5)
        p = jax.nn.softmax(s, axis=-1)
        return jnp.einsum('bgrqk,bkgd->bqgrd', p.astype(v.dtype), v)

    o = lax.map(one_block, qb)
    return o.transpose(1, 0, 2, 3, 4, 5).reshape(B, S, GQA_Q_WIDTH)


def swiglu(h, w_gate, w_up, w_down):
    return (jax.nn.silu(h @ w_gate) * (h @ w_up)) @ w_down


def routed_moe(h, router_w, router_bias, w_exp_gate, w_exp_up, w_exp_down):
    T, D = h.shape
    E = N_EXPERTS
    scores = jax.nn.sigmoid((h @ router_w).astype(jnp.float32))
    biased = scores + router_bias.astype(jnp.float32)[None, :]
    grp = biased.reshape(T, N_EXPERT_GROUPS, E // N_EXPERT_GROUPS)
    grp_score = lax.top_k(grp, 2)[0].sum(-1)
    _, grp_idx = lax.top_k(grp_score, TOPK_GROUPS)
    grp_mask = jax.nn.one_hot(grp_idx, N_EXPERT_GROUPS, dtype=jnp.float32).sum(1)
    exp_mask = jnp.repeat(grp_mask, E // N_EXPERT_GROUPS, axis=1)
    _, idx = lax.top_k(jnp.where(exp_mask > 0, biased, -jnp.inf), TOP_K)
    gate = jnp.take_along_axis(scores, idx, axis=1)
    gate = gate / jnp.sum(gate, axis=-1, keepdims=True) * ROUTED_SCALE

    A = T * TOP_K
    flat_e = idx.reshape(-1).astype(jnp.int32)
    order = jnp.argsort(flat_e)
    sorted_e = flat_e[order]
    tok_sorted = (order // TOP_K).astype(jnp.int32)
    gate_sorted = gate.reshape(-1)[order].astype(h.dtype)
    counts = jnp.bincount(flat_e, length=E).astype(jnp.int32)
    start = jnp.cumsum(counts) - counts
    padded = ((counts + MOE_BLOCK - 1) // MOE_BLOCK) * MOE_BLOCK
    pend = jnp.cumsum(padded)
    pstart = pend - padded
    dest = pstart[sorted_e] + (jnp.arange(A, dtype=jnp.int32) - start[sorted_e])
    NB = -(-A // MOE_BLOCK) + E
    buf_tok = jnp.full((NB * MOE_BLOCK,), T, dtype=jnp.int32).at[dest].set(tok_sorted)
    buf_gate = jnp.zeros((NB * MOE_BLOCK,), dtype=h.dtype).at[dest].set(gate_sorted)
    block_expert = jnp.searchsorted(pend, jnp.arange(NB, dtype=jnp.int32) * MOE_BLOCK, side='right')
    block_expert = jnp.minimum(block_expert, E - 1).astype(jnp.int32)
    h_pad = jnp.concatenate([h, jnp.zeros((1, D), h.dtype)], axis=0)

    def step(acc, blk):
        e, tok, g = blk
        rows = h_pad[tok]
        y = swiglu(rows, w_exp_gate[e], w_exp_up[e], w_exp_down[e])
        return acc.at[tok].add(y * g[:, None]), None

    acc0 = jnp.zeros((T + 1, D), h.dtype)
    acc, _ = lax.scan(step, acc0, (block_expert,
                                   buf_tok.reshape(NB, MOE_BLOCK),
                                   buf_gate.reshape(NB, MOE_BLOCK)))
    return acc[:T]


def setup_inputs(seed: int = 0) -> dict:
    key = jax.random.key(seed)
    ks = jax.random.split(key, 22)
    D = D_MODEL
    nrm = lambda k, shp, s: jax.random.normal(k, shp, jnp.float32) * s
    return {
        "x": nrm(ks[0], (BATCH, SEQ, D), 1.0),
        "c": nrm(ks[1], (BATCH, D), 1.0),
        "w_ada": nrm(ks[2], (D, 6 * D), 0.5 * D ** -0.5),
        "b_ada": nrm(ks[3], (6 * D,), 0.01),
        "norm1_g": 1.0 + nrm(ks[4], (D,), 0.02),
        "w_in": nrm(ks[5], (D, IN_WIDTH), D ** -0.5),
        "b_gate": nrm(ks[6], (N_BRANCHES * D,), 0.01),
        "qn_g": 1.0 + nrm(ks[7], (HEAD_DIM,), 0.02),
        "kn_g": 1.0 + nrm(ks[8], (HEAD_DIM,), 0.02),
        "w_o_dil": nrm(ks[9], (DIL_OUT_WIDTH, D), DIL_OUT_WIDTH ** -0.5),
        "w_o_gqa": nrm(ks[10], (GQA_Q_WIDTH, D), GQA_Q_WIDTH ** -0.5),
        "w_out": nrm(ks[11], (D, D), D ** -0.5),
        "norm2_g": 1.0 + nrm(ks[12], (D,), 0.02),
        "router_w": nrm(ks[13], (D, N_EXPERTS), D ** -0.5),
        "router_bias": nrm(ks[14], (N_EXPERTS,), 0.01),
        "w_exp_gate": nrm(ks[15], (N_EXPERTS, D, EXPERT_FF), D ** -0.5),
        "w_exp_up": nrm(ks[16], (N_EXPERTS, D, EXPERT_FF), D ** -0.5),
        "w_exp_down": nrm(ks[17], (N_EXPERTS, EXPERT_FF, D), EXPERT_FF ** -0.5),
        "w_sh_gate": nrm(ks[18], (D, SHARED_FF), D ** -0.5),
        "w_sh_up": nrm(ks[19], (D, SHARED_FF), D ** -0.5),
        "w_sh_down": nrm(ks[20], (SHARED_FF, D), SHARED_FF ** -0.5),
        "final_g": 1.0 + nrm(ks[21], (D,), 0.02),
    }


def reference(x, c, w_ada, b_ada, norm1_g, w_in, b_gate, qn_g, kn_g, w_o_dil, w_o_gqa,
              w_out, norm2_g, router_w, router_bias, w_exp_gate, w_exp_up, w_exp_down,
              w_sh_gate, w_sh_up, w_sh_down, final_g):
    B, S, D = x.shape
    hd = HEAD_DIM
    rows = S // GRID_W
    t = jnp.arange(S, dtype=jnp.int32)
    cos1, sin1 = rope_cos_sin(t, hd)
    row_pos = jnp.repeat(jnp.arange(rows, dtype=jnp.int32), GRID_W)
    col_pos = jnp.tile(jnp.arange(GRID_W, dtype=jnp.int32), rows)
    cos_r, sin_r = rope_cos_sin(row_pos, hd // 2)
    cos_c, sin_c = rope_cos_sin(col_pos, hd // 2)

    for _ in range(DEPTH):
        mod = (jax.nn.silu(c) @ w_ada + b_ada).reshape(B, 6, D)
        shift1, scale1, gate1, shift2, scale2, gate2 = [mod[:, i][:, None, :] for i in range(6)]

        h = rms_norm(x, norm1_g) * (1.0 + scale1) + shift1
        proj = h @ w_in
        o0 = 0
        qa = proj[..., o0:o0 + DIL_WIDTH]; o0 += DIL_WIDTH
        ka = proj[..., o0:o0 + DIL_WIDTH]; o0 += DIL_WIDTH
        va = proj[..., o0:o0 + DIL_WIDTH]; o0 += DIL_WIDTH
        qb = proj[..., o0:o0 + GQA_Q_WIDTH]; o0 += GQA_Q_WIDTH
        kb = proj[..., o0:o0 + GQA_KV_WIDTH]; o0 += GQA_KV_WIDTH
        vb = proj[..., o0:o0 + GQA_KV_WIDTH]; o0 += GQA_KV_WIDTH
        gates = jax.nn.sigmoid(proj[..., o0:o0 + N_BRANCHES * D] + b_gate)
        g_a, g_b = gates[..., :D], gates[..., D:]

        dshape = (B, S, N_DIL_GROUPS, DIL_HEADS, hd)
        qa = apply_rope(qa.reshape(dshape), cos1, sin1)
        ka = apply_rope(ka.reshape(dshape), cos1, sin1)
        va = va.reshape(dshape)
        y_a = dilated_mixture(qa, ka, va) @ w_o_dil

        qb = rms_norm(qb.reshape(B, S, GQA_Q_HEADS, hd), qn_g)
        kb = rms_norm(kb.reshape(B, S, GQA_KV_HEADS, hd), kn_g)
        vb = vb.reshape(B, S, GQA_KV_HEADS, hd)
        half = hd // 2
        qb = jnp.concatenate([apply_rope(qb[..., :half], cos_r, sin_r),
                              apply_rope(qb[..., half:], cos_c, sin_c)], axis=-1)
        kb = jnp.concatenate([apply_rope(kb[..., :half], cos_r, sin_r),
                              apply_rope(kb[..., half:], cos_c, sin_c)], axis=-1)
        y_b = gqa_blocked(qb, kb, vb) @ w_o_gqa

        mixed = (g_a * y_a + g_b * y_b) @ w_out
        x = x + gate1 * mixed

        h2 = (rms_norm(x, norm2_g) * (1.0 + scale2) + shift2).reshape(B * S, D)
        ffn = swiglu(h2, w_sh_gate, w_sh_up, w_sh_down) + routed_moe(
            h2, router_w, router_bias, w_exp_gate, w_exp_up, w_exp_down)
        x = x + gate2 * ffn.reshape(B, S, D)

    return rms_norm(x, final_g)
```

```python
import functools

import jax
import jax.numpy as jnp
from jax import lax
from jax.experimental import pallas as pl
from jax.experimental.pallas import tpu as pltpu

F32 = jnp.float32
BF16 = jnp.bfloat16
U32 = jnp.uint32
I32 = jnp.int32

D_MODEL = 1024
HEAD_DIM = 64
DIL_CONFIGS = ((128, 1), (512, 4), (2048, 16))
N_DIL_GROUPS = 3
DIL_GROUP_WIDTH = 256
DIL_WIDTH = N_DIL_GROUPS * DIL_GROUP_WIDTH
DIL_HALF_WINDOW = 64
GQA_Q_WIDTH = 1024
GQA_KV_WIDTH = 256
GQA_KV_HEADS = 4
GQA_REP = 4
GRID_W = 64
ROPE_THETA = 10000.0
N_EXPERTS = 256
TOP_K = 8
N_EXPERT_GROUPS = 8
GROUP_SIZE = N_EXPERTS // N_EXPERT_GROUPS
TOPK_GROUPS = 4
EXPERT_FF = 256
ROUTED_SCALE = 2.5
NORM_EPS = 1e-6
HALF = D_MODEL // 2

LANES = 128
VMEM_LIMIT = 56 * 1024 * 1024

TM_PROJ = 256
TQ_GQA = 256
QB_DIL = 128
TR_ROUTE = 512
TM_DISPATCH = 256
TM_COMBINE = 128
BM_EXPERT = 512

NEG_INF = float("-inf")


def _params(sem):
    return pltpu.CompilerParams(dimension_semantics=sem, vmem_limit_bytes=VMEM_LIMIT)


def _const_spec(shape):
    n = len(shape)
    return pl.BlockSpec(shape, lambda *_: (0,) * n)


def _adaln_kernel(c_ref, w_ref, b_ref, o_ref):
    c = c_ref[...]
    a = c * jax.nn.sigmoid(c)
    o_ref[...] = jnp.dot(a, w_ref[...], preferred_element_type=F32,
                         precision=lax.Precision.HIGHEST) + b_ref[...]


def _adaln(c, w_ada, b_ada):
    B = c.shape[0]
    return pl.pallas_call(
        _adaln_kernel,
        grid=(6,),
        in_specs=[pl.BlockSpec((B, D_MODEL), lambda j: (0, 0)),
                  pl.BlockSpec((D_MODEL, D_MODEL), lambda j: (0, j)),
                  pl.BlockSpec((1, D_MODEL), lambda j: (0, j))],
        out_specs=pl.BlockSpec((B, D_MODEL), lambda j: (0, j)),
        out_shape=jax.ShapeDtypeStruct((B, 6 * D_MODEL), F32),
        compiler_params=_params(("arbitrary",)),
        name="adaln",
    )(c, w_ada, b_ada.reshape(1, 6 * D_MODEL))


def _tile_lanes(t, width):
    return jnp.concatenate([t] * (width // LANES), axis=1)


def _rope(v, cos, sin_signed, half):
    w = v.shape[-1]
    lane = lax.broadcasted_iota(I32, (1, w), 1)
    first = (lane % (2 * half)) < half
    up = pltpu.roll(v, w - half, axis=1)
    dn = pltpu.roll(v, half, axis=1)
    return v * _tile_lanes(cos, w) + jnp.where(first, up, dn) * _tile_lanes(sin_signed, w)


def _head_rms(v, gain, bd_ref):
    w = v.shape[-1]
    parts = []
    for c0 in range(0, w, 256):
        vc = v[:, c0:c0 + 256]
        ms = jnp.dot((vc * vc).astype(BF16), bd_ref[...], preferred_element_type=F32) * (1.0 / HEAD_DIM)
        parts.append(vc * lax.rsqrt(ms + NORM_EPS))
    return jnp.concatenate(parts, axis=1) * gain


def _in_proj_kernel(x_ref, mod_ref, g1_ref, wqa_ref, wka_ref, wva_ref, wqb_ref, wkb_ref, wvb_ref,
                    wga_ref, wgb_ref, bg_ref, qn_ref, kn_ref, cos1_ref, sin1_ref, cosb_ref, sinb_ref,
                    bd_ref, qa_ref, ka_ref, va_ref, qb_ref, kbt_ref, vb_ref, ga_ref, gb_ref):
    x = x_ref[0]
    ms = jnp.mean(x * x, axis=-1, keepdims=True)
    h = x * lax.rsqrt(ms + NORM_EPS) * g1_ref[...]
    h = h * (1.0 + mod_ref[0, 1:2, :]) + mod_ref[0, 0:1, :]
    hb = h.astype(BF16)

    def proj(w_ref):
        return jnp.dot(hb, w_ref[...], preferred_element_type=F32)

    cos1, sin1 = cos1_ref[...], sin1_ref[...]
    cosb, sinb = cosb_ref[...], sinb_ref[...]

    qa = _rope(proj(wqa_ref), cos1, sin1, HEAD_DIM // 2) * (HEAD_DIM ** -0.5)
    ka = _rope(proj(wka_ref), cos1, sin1, HEAD_DIM // 2)
    va = proj(wva_ref)
    for g in range(N_DIL_GROUPS):
        sl = slice(g * DIL_GROUP_WIDTH, (g + 1) * DIL_GROUP_WIDTH)
        qa_ref[g, 0] = qa[:, sl].astype(BF16)
        ka_ref[g, 0] = ka[:, sl].astype(BF16)
        va_ref[g, 0] = va[:, sl].astype(BF16)

    qb = _head_rms(proj(wqb_ref), qn_ref[...], bd_ref)
    qb_ref[0] = (_rope(qb, cosb, sinb, HEAD_DIM // 4) * (HEAD_DIM ** -0.5)).astype(BF16)
    kb = _rope(_head_rms(proj(wkb_ref), kn_ref[...], bd_ref), cosb, sinb, HEAD_DIM // 4)
    kbt_ref[0] = kb.T.astype(BF16)
    vb_ref[0] = proj(wvb_ref).astype(BF16)

    ga_ref[0] = jax.nn.sigmoid(proj(wga_ref) + bg_ref[:, :D_MODEL]).astype(BF16)
    gb_ref[0] = jax.nn.sigmoid(proj(wgb_ref) + bg_ref[:, D_MODEL:]).astype(BF16)


def _in_proj(x, mod, g1, weights, bgate, qn, kn, tables, bd):
    B, S, _ = x.shape
    tm = TM_PROJ
    tok = lambda w: pl.BlockSpec((1, tm, w), lambda b, i: (b, i, 0))
    tab = pl.BlockSpec((tm, LANES), lambda b, i: (i, 0))
    dil_out = pl.BlockSpec((N_DIL_GROUPS, 1, tm, DIL_GROUP_WIDTH), lambda b, i: (0, b, i, 0))
    dil_shape = jax.ShapeDtypeStruct((N_DIL_GROUPS, B, S, DIL_GROUP_WIDTH), BF16)
    in_specs = ([tok(D_MODEL), pl.BlockSpec((1, 6, D_MODEL), lambda b, i: (b, 0, 0)), _const_spec((1, D_MODEL))]
                + [_const_spec(w.shape) for w in weights]
                + [_const_spec(bgate.shape), _const_spec(qn.shape), _const_spec(kn.shape), tab, tab, tab, tab,
                   _const_spec(bd.shape)])
    out_specs = [dil_out, dil_out, dil_out, tok(GQA_Q_WIDTH),
                 pl.BlockSpec((1, GQA_KV_WIDTH, tm), lambda b, i: (b, 0, i)), tok(GQA_KV_WIDTH),
                 tok(D_MODEL), tok(D_MODEL)]
    out_shape = [dil_shape, dil_shape, dil_shape,
                 jax.ShapeDtypeStruct((B, S, GQA_Q_WIDTH), BF16),
                 jax.ShapeDtypeStruct((B, GQA_KV_WIDTH, S), BF16),
                 jax.ShapeDtypeStruct((B, S, GQA_KV_WIDTH), BF16),
                 jax.ShapeDtypeStruct((B, S, D_MODEL), BF16),
                 jax.ShapeDtypeStruct((B, S, D_MODEL), BF16)]
    return pl.pallas_call(
        _in_proj_kernel,
        grid=(B, S // tm),
        in_specs=in_specs,
        out_specs=out_specs,
        out_shape=out_shape,
        compiler_params=_params(("arbitrary", "arbitrary")),
        name="in_proj",
    )(x, mod, g1, *weights, bgate, qn, kn, *tables, bd)


def _dil_kernel(q_ref, k_ref, v_ref, o_ref, lse_ref, *, dil, seq):
    n = DIL_HALF_WINDOW
    qb = min(QB_DIL, seq)
    kw = min(qb + 2 * n, seq)
    lane = lax.broadcasted_iota(I32, (1, DIL_GROUP_WIDTH), 1)
    head_masks = [(lane >= h * HEAD_DIM) & (lane < (h + 1) * HEAD_DIM) for h in range(4)]

    for r in range(dil):
        cols = slice(r * DIL_GROUP_WIDTH, (r + 1) * DIL_GROUP_WIDTH)

        def block(i, carry, cols=cols):
            a = pl.multiple_of(i * qb, qb)
            ws = pl.multiple_of(jnp.clip(a - n, 0, seq - kw), n)
            q = q_ref[pl.ds(a, qb), cols]
            k = k_ref[pl.ds(ws, kw), cols]
            v = v_ref[pl.ds(ws, kw), cols]
            qpos = a + lax.broadcasted_iota(I32, (qb, 1), 0)
            kpos = ws + lax.broadcasted_iota(I32, (1, kw), 1)
            valid = jnp.abs(kpos - qpos) <= n
            o_acc = jnp.zeros((qb, DIL_GROUP_WIDTH), F32)
            lse_acc = jnp.zeros((qb, DIL_GROUP_WIDTH), F32)
            for hm in head_masks:
                qm = jnp.where(hm, q, jnp.zeros_like(q))
                s = lax.dot_general(qm, k, (((1,), (1,)), ((), ())), preferred_element_type=F32)
                s = jnp.where(valid, s, NEG_INF)
                m = jnp.max(s, axis=1, keepdims=True)
                p = jnp.exp(s - m)
                l = jnp.sum(p, axis=1, keepdims=True)
                pv = jnp.dot(p.astype(BF16), v, preferred_element_type=F32)
                o_acc = jnp.where(hm, pv * (1.0 / l), o_acc)
                lse_acc = jnp.where(hm, m + jnp.log(l), lse_acc)
            o_ref[pl.ds(a, qb), cols] = o_acc.astype(BF16)
            lse_ref[pl.ds(a, qb), cols] = lse_acc
            return carry

        lax.fori_loop(0, seq // qb, block, 0)


def _dilated(qa, ka, va, g):
    _, B, S, W = qa.shape
    dil = DIL_CONFIGS[g][1]
    seq = S // dil
    view = lambda a: a.reshape(N_DIL_GROUPS, B, seq, dil * W)
    in_spec = pl.BlockSpec((None, None, seq, dil * W), lambda b: (g, b, 0, 0))
    out_spec = pl.BlockSpec((None, seq, dil * W), lambda b: (b, 0, 0))
    o, lse = pl.pallas_call(
        functools.partial(_dil_kernel, dil=dil, seq=seq),
        grid=(B,),
        in_specs=[in_spec, in_spec, in_spec],
        out_specs=[out_spec, out_spec],
        out_shape=[jax.ShapeDtypeStruct((B, seq, dil * W), BF16), jax.ShapeDtypeStruct((B, seq, dil * W), F32)],
        compiler_params=_params(("arbitrary",)),
        name=f"dilated{g}",
    )(view(qa), view(ka), view(va))
    return o.reshape(B, S, W), lse.reshape(B, S, W)


def _gqa_kernel(q_ref, kt_ref, v_ref, o_ref):
    tq = q_ref.shape[0]
    lane = lax.broadcasted_iota(I32, (1, GQA_KV_WIDTH), 1)
    outs = [jnp.zeros((tq, GQA_KV_WIDTH), F32) for _ in range(GQA_REP)]
    for g in range(GQA_KV_HEADS):
        hm = (lane >= g * HEAD_DIM) & (lane < (g + 1) * HEAD_DIM)
        qg = jnp.concatenate(
            [jnp.where(hm, q_ref[:, c * 256:(c + 1) * 256], jnp.zeros((tq, 256), BF16)) for c in range(GQA_REP)],
            axis=0)
        s = jnp.dot(qg, kt_ref[...], preferred_element_type=F32)
        m = jnp.max(s, axis=1, keepdims=True)
        p = jnp.exp(s - m)
        l = jnp.sum(p, axis=1, keepdims=True)
        pv = jnp.dot(p.astype(BF16), v_ref[...], preferred_element_type=F32) * (1.0 / l)
        for c in range(GQA_REP):
            outs[c] = jnp.where(hm, pv[c * tq:(c + 1) * tq], outs[c])
    for c in range(GQA_REP):
        o_ref[:, c * 256:(c + 1) * 256] = outs[c].astype(BF16)


def _gqa(qb, kbt, vb):
    B, S, _ = qb.shape
    tq = TQ_GQA
    return pl.pallas_call(
        _gqa_kernel,
        grid=(B, S // tq),
        in_specs=[pl.BlockSpec((None, tq, GQA_Q_WIDTH), lambda b, i: (b, i, 0)),
                  pl.BlockSpec((None, GQA_KV_WIDTH, S), lambda b, i: (b, 0, 0)),
                  pl.BlockSpec((None, S, GQA_KV_WIDTH), lambda b, i: (b, 0, 0))],
        out_specs=pl.BlockSpec((None, tq, GQA_Q_WIDTH), lambda b, i: (b, i, 0)),
        out_shape=jax.ShapeDtypeStruct((B, S, GQA_Q_WIDTH), BF16),
        compiler_params=_params(("arbitrary", "arbitrary")),
        name="gqa",
    )(qb, kbt, vb)


def _pack_halves(y):
    lo = lax.bitcast_convert_type(y[:, :HALF].astype(BF16).astype(F32), U32)
    hi = lax.bitcast_convert_type(y[:, HALF:].astype(BF16).astype(F32), U32)
    return (lo >> 16) | hi


def _unpack_halves(u):
    lo = lax.bitcast_convert_type(u << 16, F32)
    hi = lax.bitcast_convert_type(u & jnp.uint32(0xFFFF0000), F32)
    return lo, hi


def _mix_kernel(x_ref, o0_ref, o1_ref, o2_ref, l0_ref, l1_ref, l2_ref, yb_ref, ga_ref, gb_ref, mod_ref,
                wod_ref, wog_ref, wout_ref, g2_ref, rw_ref, wsg_ref, wsu_ref, wsd_ref,
                xs_ref, h2_ref, logit_ref):
    l0, l1, l2 = l0_ref[0], l1_ref[0], l2_ref[0]
    mx = jnp.maximum(jnp.maximum(l0, l1), l2)
    e0, e1, e2 = jnp.exp(l0 - mx), jnp.exp(l1 - mx), jnp.exp(l2 - mx)
    ya = (e0 * o0_ref[0].astype(F32) + e1 * o1_ref[0].astype(F32) + e2 * o2_ref[0].astype(F32)) / (e0 + e1 + e2)
    y_a = jnp.dot(ya.astype(BF16), wod_ref[...], preferred_element_type=F32)
    y_b = jnp.dot(yb_ref[0], wog_ref[...], preferred_element_type=F32)
    mix = ga_ref[0].astype(F32) * y_a + gb_ref[0].astype(F32) * y_b
    mixed = jnp.dot(mix.astype(BF16), wout_ref[...], preferred_element_type=F32)
    x1 = x_ref[0] + mod_ref[0, 2:3, :] * mixed

    ms = jnp.mean(x1 * x1, axis=-1, keepdims=True)
    h2 = x1 * lax.rsqrt(ms + NORM_EPS) * g2_ref[...]
    h2 = h2 * (1.0 + mod_ref[0, 4:5, :]) + mod_ref[0, 3:4, :]
    h2b = h2.astype(BF16)
    h2_ref[0] = _pack_halves(h2)
    logit_ref[0] = jnp.dot(h2b, rw_ref[...], preferred_element_type=F32)

    gt = jnp.dot(h2b, wsg_ref[...], preferred_element_type=F32)
    up = jnp.dot(h2b, wsu_ref[...], preferred_element_type=F32)
    act = (gt * jax.nn.sigmoid(gt) * up).astype(BF16)
    shared = jnp.dot(act, wsd_ref[...], preferred_element_type=F32)
    xs_ref[0] = x1 + mod_ref[0, 5:6, :] * shared


def _mix(x, dil_outs, yb, ga, gb, mod, weights):
    B, S, _ = x.shape
    tm = TM_PROJ
    tok = lambda w: pl.BlockSpec((1, tm, w), lambda b, i: (b, i, 0))
    (o0, l0), (o1, l1), (o2, l2) = dil_outs
    in_specs = ([tok(D_MODEL)] + [tok(DIL_GROUP_WIDTH)] * 6 + [tok(GQA_Q_WIDTH), tok(D_MODEL), tok(D_MODEL),
                pl.BlockSpec((1, 6, D_MODEL), lambda b, i: (b, 0, 0))] + [_const_spec(w.shape) for w in weights])
    return pl.pallas_call(
        _mix_kernel,
        grid=(B, S // tm),
        in_specs=in_specs,
        out_specs=[tok(D_MODEL), tok(HALF), tok(N_EXPERTS)],
        out_shape=[jax.ShapeDtypeStruct((B, S, D_MODEL), F32),
                   jax.ShapeDtypeStruct((B, S, HALF), U32),
                   jax.ShapeDtypeStruct((B, S, N_EXPERTS), F32)],
        compiler_params=_params(("arbitrary", "arbitrary")),
        name="mix",
    )(x, o0, o1, o2, l0, l1, l2, yb, ga, gb, mod, *weights)


def _first_argmax(v, idx, big):
    m = jnp.max(v, axis=0, keepdims=True)
    i = jnp.min(jnp.where(v == m, idx, big), axis=0, keepdims=True)
    return m, i


def _route_kernel(logit_ref, bias_ref, tri_ref, idx_ref, gate_ref, gate_tm_ref, rank_ref, cnt_ref, base_ref):
    step = pl.program_id(0)
    tr = logit_ref.shape[0]

    @pl.when(step == 0)
    def _():
        base_ref[...] = jnp.zeros_like(base_ref)

    scores = jax.nn.sigmoid(logit_ref[...].T)
    biased = scores + bias_ref[...]
    row = lax.broadcasted_iota(I32, (N_EXPERTS, tr), 0)

    gscore = []
    group_row = lax.broadcasted_iota(I32, (GROUP_SIZE, tr), 0)
    for g in range(N_EXPERT_GROUPS):
        v, ri = biased[g * GROUP_SIZE:(g + 1) * GROUP_SIZE], group_row + g * GROUP_SIZE
        m1, i1 = _first_argmax(v, ri, N_EXPERTS)
        m2 = jnp.max(jnp.where(ri == i1, NEG_INF, v), axis=0, keepdims=True)
        gscore.append(m1 + m2)
    cur = jnp.concatenate(gscore, axis=0)
    gi = lax.broadcasted_iota(I32, (N_EXPERT_GROUPS, tr), 0)
    sel = jnp.zeros((N_EXPERT_GROUPS, tr), F32)
    for _ in range(TOPK_GROUPS):
        _, i = _first_argmax(cur, gi, N_EXPERT_GROUPS)
        hit = gi == i
        sel = jnp.where(hit, 1.0, sel)
        cur = jnp.where(hit, NEG_INF, cur)

    cur = jnp.concatenate(
        [jnp.where(sel[g:g + 1] > 0, biased[g * GROUP_SIZE:(g + 1) * GROUP_SIZE], NEG_INF)
         for g in range(N_EXPERT_GROUPS)], axis=0)
    idxs, gates, hits = [], [], []
    assigned = jnp.zeros((N_EXPERTS, tr), F32)
    for _ in range(TOP_K):
        _, i = _first_argmax(cur, row, N_EXPERTS)
        hit = row == i
        gates.append(jnp.sum(jnp.where(hit, scores, 0.0), axis=0, keepdims=True))
        cur = jnp.where(hit, NEG_INF, cur)
        assigned = jnp.where(hit, 1.0, assigned)
        idxs.append(i)
        hits.append(hit)
    gate = jnp.concatenate(gates, axis=0)
    gate = gate / jnp.sum(gate, axis=0, keepdims=True) * ROUTED_SCALE

    before = jnp.dot(assigned.astype(BF16), tri_ref[...], preferred_element_type=F32) + base_ref[:, 0:1]
    ranks = [jnp.sum(jnp.where(h, before, 0.0), axis=0, keepdims=True) for h in hits]

    idx_ref[...] = jnp.concatenate(idxs, axis=0)
    gate_ref[...] = gate
    rank_ref[...] = jnp.concatenate(ranks, axis=0).astype(I32)
    gate_tm_ref[...] = jnp.concatenate([gate, jnp.zeros((LANES - TOP_K, tr), F32)], axis=0).T
    base_ref[...] = base_ref[...] + jnp.sum(assigned, axis=1, keepdims=True)
    cnt_ref[...] = base_ref[...]


def _route(logits, bias):
    T = logits.shape[0]
    tr = TR_ROUTE
    tri = (lax.broadcasted_iota(I32, (tr, tr), 0) < lax.broadcasted_iota(I32, (tr, tr), 1)).astype(BF16)
    slot = pl.BlockSpec((TOP_K, tr), lambda i: (0, i))
    return pl.pallas_call(
        _route_kernel,
        grid=(T // tr,),
        in_specs=[pl.BlockSpec((tr, N_EXPERTS), lambda i: (i, 0)), _const_spec((N_EXPERTS, 1)), _const_spec((tr, tr))],
        out_specs=[slot, slot, pl.BlockSpec((tr, LANES), lambda i: (i, 0)), slot, _const_spec((N_EXPERTS, LANES))],
        out_shape=[jax.ShapeDtypeStruct((TOP_K, T), I32), jax.ShapeDtypeStruct((TOP_K, T), F32),
                   jax.ShapeDtypeStruct((T, LANES), F32), jax.ShapeDtypeStruct((TOP_K, T), I32),
                   jax.ShapeDtypeStruct((N_EXPERTS, LANES), F32)],
        scratch_shapes=[pltpu.VMEM((N_EXPERTS, LANES), F32)],
        compiler_params=_params(("arbitrary",)),
        name="route",
    )(logits, bias.reshape(N_EXPERTS, 1), tri)


def _plan_kernel(cnt_ref, idx_ref, rank_ref, ltri_ref, dest_ref, bexp_ref, nblk_ref, *, nb_max):
    tr = idx_ref.shape[1]
    blocks = jnp.floor((cnt_ref[...] + (BM_EXPERT - 1)) * (1.0 / BM_EXPERT))
    start = jnp.dot(ltri_ref[...], blocks.astype(BF16), preferred_element_type=F32)
    end = start + blocks
    blk = lax.broadcasted_iota(I32, (1, nb_max), 1).astype(F32)
    bexp = jnp.sum(jnp.where(end[:, 0:1] <= blk, 1.0, 0.0), axis=0, keepdims=True)
    bexp_ref[...] = jnp.minimum(bexp, N_EXPERTS - 1.0).astype(I32)
    nblk_ref[...] = jnp.sum(blocks, axis=0, keepdims=True).astype(I32)

    row = lax.broadcasted_iota(I32, (N_EXPERTS, tr), 0)
    start_rows = start[:, 0:1] * float(BM_EXPERT)
    dests = []
    for k in range(TOP_K):
        hit = row == idx_ref[k:k + 1, :]
        dests.append(jnp.sum(jnp.where(hit, start_rows, 0.0), axis=0, keepdims=True))
    dest_ref[...] = jnp.concatenate(dests, axis=0).astype(I32) + rank_ref[...]


def _plan(counts, idx, rank, nb_max):
    T = idx.shape[1]
    tr = TR_ROUTE
    ltri = (lax.broadcasted_iota(I32, (N_EXPERTS, N_EXPERTS), 1)
            < lax.broadcasted_iota(I32, (N_EXPERTS, N_EXPERTS), 0)).astype(BF16)
    slot = pl.BlockSpec((TOP_K, tr), lambda i: (0, i))
    return pl.pallas_call(
        functools.partial(_plan_kernel, nb_max=nb_max),
        grid=(T // tr,),
        in_specs=[_const_spec((N_EXPERTS, LANES)), slot, slot, _const_spec((N_EXPERTS, N_EXPERTS))],
        out_specs=[slot, _const_spec((1, nb_max)), _const_spec((1, LANES))],
        out_shape=[jax.ShapeDtypeStruct((TOP_K, T), I32), jax.ShapeDtypeStruct((1, nb_max), I32),
                   jax.ShapeDtypeStruct((1, LANES), I32)],
        compiler_params=_params(("arbitrary",)),
        name="plan",
    )(counts, idx, rank, ltri)


def _dispatch_kernel(dest_ref, h_ref, xs_in_ref, xs_ref, sem):
    del xs_in_ref
    tm = h_ref.shape[0]

    def row_copy(t, k):
        return pltpu.make_async_copy(h_ref.at[pl.ds(t, 1), :], xs_ref.at[pl.ds(dest_ref[k, t], 1), :], sem)

    def issue(t, carry):
        for k in range(TOP_K):
            row_copy(t, k).start()
        return carry

    lax.fori_loop(0, tm, issue, 0)
    for _ in range(TOP_K):
        pltpu.make_async_copy(h_ref, xs_ref.at[pl.ds(0, tm), :], sem).wait()


def _dispatch(dest, h2p, rows_padded):
    T = h2p.shape[0]
    tm = TM_DISPATCH
    xs0 = jnp.zeros((rows_padded, HALF), U32)
    return pl.pallas_call(
        _dispatch_kernel,
        grid=(T // tm,),
        in_specs=[pl.BlockSpec((TOP_K, tm), lambda i: (0, i), memory_space=pltpu.SMEM),
                  pl.BlockSpec((tm, HALF), lambda i: (i, 0)),
                  pl.BlockSpec(memory_space=pl.ANY)],
        out_specs=pl.BlockSpec(memory_space=pl.ANY),
        out_shape=jax.ShapeDtypeStruct((rows_padded, HALF), U32),
        scratch_shapes=[pltpu.SemaphoreType.DMA(())],
        input_output_aliases={2: 0},
        compiler_params=_params(("arbitrary",)),
        name="dispatch",
    )(dest, h2p, xs0)


def _expert_kernel(bexp_ref, nblk_ref, xs_ref, wg_ref, wu_ref, wd_ref, ys_ref, wgb_ref, wub_ref, wdb_ref):
    p = pl.program_id(0)
    prev = bexp_ref[jnp.maximum(p - 1, 0)]
    active = p < nblk_ref[0]

    @pl.when(active & ((p == 0) | (bexp_ref[p] != prev)))
    def _():
        wgb_ref[...] = wg_ref[...].astype(BF16)
        wub_ref[...] = wu_ref[...].astype(BF16)
        wdb_ref[...] = wd_ref[...].astype(BF16)

    @pl.when(active)
    def _():
        lo, hi = _unpack_halves(xs_ref[...])
        lo, hi = lo.astype(BF16), hi.astype(BF16)

        def proj(w_ref):
            return (jnp.dot(lo, w_ref[:HALF, :], preferred_element_type=F32)
                    + jnp.dot(hi, w_ref[HALF:, :], preferred_element_type=F32))

        gt, up = proj(wgb_ref), proj(wub_ref)
        act = (gt * jax.nn.sigmoid(gt) * up).astype(BF16)
        ys_ref[...] = _pack_halves(jnp.dot(act, wdb_ref[...], preferred_element_type=F32))


def _experts(bexp, nblk, xs, w_gate, w_up, w_down, nb_max):
    bm = BM_EXPERT
    rows = lambda p, be, nb: (jnp.minimum(p, nb[0] - 1), 0)
    wspec = lambda shape: pl.BlockSpec((None,) + shape, lambda p, be, nb: (be[p], 0, 0))
    grid_spec = pltpu.PrefetchScalarGridSpec(
        num_scalar_prefetch=2,
        grid=(nb_max,),
        in_specs=[pl.BlockSpec((bm, HALF), rows), wspec((D_MODEL, EXPERT_FF)), wspec((D_MODEL, EXPERT_FF)),
                  wspec((EXPERT_FF, D_MODEL))],
        out_specs=pl.BlockSpec((bm, HALF), rows),
        scratch_shapes=[pltpu.VMEM((D_MODEL, EXPERT_FF), BF16), pltpu.VMEM((D_MODEL, EXPERT_FF), BF16),
                        pltpu.VMEM((EXPERT_FF, D_MODEL), BF16)],
    )
    return pl.pallas_call(
        _expert_kernel,
        grid_spec=grid_spec,
        out_shape=jax.ShapeDtypeStruct(xs.shape, U32),
        input_output_aliases={2: 0},
        compiler_params=_params(("arbitrary",)),
        name="experts",
    )(bexp, nblk, xs, w_gate, w_up, w_down)


def _combine_kernel(dest_ref, dnext_ref, xs_ref, gate_ref, mod_ref, fg_ref, ys_ref, o_ref, buf_ref, sem):
    i = pl.program_id(0)
    n = pl.num_programs(0)
    tm = o_ref.shape[0]
    slot = i % 2

    def issue(d_ref, s):
        def body(t, carry):
            for k in range(TOP_K):
                pltpu.make_async_copy(ys_ref.at[pl.ds(d_ref[k, t], 1), :],
                                      buf_ref.at[s, k, pl.ds(t, 1), :], sem.at[s]).start()
            return carry
        lax.fori_loop(0, tm, body, 0)

    @pl.when(i == 0)
    def _():
        issue(dest_ref, 0)

    @pl.when(i + 1 < n)
    def _():
        issue(dnext_ref, 1 - slot)

    for k in range(TOP_K):
        pltpu.make_async_copy(ys_ref.at[pl.ds(0, tm), :], buf_ref.at[slot, k], sem.at[slot]).wait()

    acc_lo = jnp.zeros((tm, HALF), F32)
    acc_hi = jnp.zeros((tm, HALF), F32)
    for k in range(TOP_K):
        lo, hi = _unpack_halves(buf_ref[slot, k])
        g = gate_ref[:, k:k + 1]
        acc_lo = acc_lo + g * lo
        acc_hi = acc_hi + g * hi
    routed = jnp.concatenate([acc_lo, acc_hi], axis=1)
    x2 = xs_ref[...] + mod_ref[0, 5:6, :] * routed
    ms = jnp.mean(x2 * x2, axis=-1, keepdims=True)
    o_ref[...] = x2 * lax.rsqrt(ms + NORM_EPS) * fg_ref[...]


def _combine(dest, xs_mid, gate_tm, mod, final_g, ys, seq):
    T = xs_mid.shape[0]
    tm = TM_COMBINE
    n = T // tm
    per_seq = seq // tm
    return pl.pallas_call(
        _combine_kernel,
        grid=(n,),
        in_specs=[pl.BlockSpec((TOP_K, tm), lambda i: (0, i), memory_space=pltpu.SMEM),
                  pl.BlockSpec((TOP_K, tm), lambda i: (0, jnp.minimum(i + 1, n - 1)), memory_space=pltpu.SMEM),
                  pl.BlockSpec((tm, D_MODEL), lambda i: (i, 0)),
                  pl.BlockSpec((tm, LANES), lambda i: (i, 0)),
                  pl.BlockSpec((1, 6, D_MODEL), lambda i: (i // per_seq, 0, 0)),
                  _const_spec((1, D_MODEL)),
                  pl.BlockSpec(memory_space=pl.ANY)],
        out_specs=pl.BlockSpec((tm, D_MODEL), lambda i: (i, 0)),
        out_shape=jax.ShapeDtypeStruct((T, D_MODEL), F32),
        scratch_shapes=[pltpu.VMEM((2, TOP_K, tm, HALF), U32), pltpu.SemaphoreType.DMA((2,))],
        compiler_params=_params(("arbitrary",)),
        name="combine",
    )(dest, dest, xs_mid, gate_tm, mod, final_g, ys)


def _rope_tables(seq):
    def cos_sin(pos, dim):
        inv = ROPE_THETA ** (-jnp.arange(0, dim, 2, dtype=F32) / dim)
        ang = pos.astype(F32)[:, None] * inv[None, :]
        ang = jnp.concatenate([ang, ang], axis=-1)
        sign = jnp.where(jnp.arange(dim) < dim // 2, -1.0, 1.0).astype(F32)
        return jnp.cos(ang), jnp.sin(ang) * sign

    t = jnp.arange(seq, dtype=I32)
    cos1, sin1 = cos_sin(t, HEAD_DIM)
    cos_r, sin_r = cos_sin(t // GRID_W, HEAD_DIM // 2)
    cos_c, sin_c = cos_sin(t % GRID_W, HEAD_DIM // 2)
    cosb = jnp.concatenate([cos_r, cos_c], axis=-1)
    sinb = jnp.concatenate([sin_r, sin_c], axis=-1)
    rep = LANES // HEAD_DIM
    return tuple(jnp.tile(a, (1, rep)) for a in (cos1, sin1, cosb, sinb))


def kernel(x, c, w_ada, b_ada, norm1_g, w_in, b_gate, qn_g, kn_g, w_o_dil, w_o_gqa, w_out, norm2_g,
           router_w, router_bias, w_exp_gate, w_exp_up, w_exp_down, w_sh_gate, w_sh_up, w_sh_down, final_g):
    B, S, D = x.shape
    T = B * S
    assert D == D_MODEL and S % (TM_PROJ * 1) == 0 and S // 16 >= QB_DIL
    row = lambda v: v.reshape(1, -1)

    mod = _adaln(c, w_ada, b_ada).reshape(B, 6, D)

    wb = w_in.astype(BF16)
    offs = [0, DIL_WIDTH, 2 * DIL_WIDTH, 3 * DIL_WIDTH, 3 * DIL_WIDTH + GQA_Q_WIDTH,
            3 * DIL_WIDTH + GQA_Q_WIDTH + GQA_KV_WIDTH, 3 * DIL_WIDTH + GQA_Q_WIDTH + 2 * GQA_KV_WIDTH]
    w_qa, w_ka, w_va = (wb[:, offs[i]:offs[i + 1]] for i in range(3))
    w_qb = wb[:, offs[3]:offs[4]].reshape(D, GQA_KV_HEADS, GQA_REP, HEAD_DIM).transpose(0, 2, 1, 3).reshape(D, GQA_Q_WIDTH)
    w_kb, w_vb = wb[:, offs[4]:offs[5]], wb[:, offs[5]:offs[6]]
    w_ga, w_gb = wb[:, offs[6]:offs[6] + D], wb[:, offs[6] + D:offs[6] + 2 * D]
    w_og = w_o_gqa.reshape(GQA_KV_HEADS, GQA_REP, HEAD_DIM, D).transpose(1, 0, 2, 3).reshape(GQA_Q_WIDTH, D).astype(BF16)

    lane_head = lax.broadcasted_iota(I32, (256, 256), 0) // HEAD_DIM
    bd = (lane_head == lane_head.T).astype(BF16)
    qn = jnp.tile(row(qn_g), (1, GQA_Q_WIDTH // HEAD_DIM))
    kn = jnp.tile(row(kn_g), (1, GQA_KV_WIDTH // HEAD_DIM))

    qa, ka, va, qb, kbt, vb, ga, gb = _in_proj(
        x, mod, row(norm1_g), (w_qa, w_ka, w_va, w_qb, w_kb, w_vb, w_ga, w_gb), row(b_gate), qn, kn,
        _rope_tables(S), bd)

    dil_outs = [_dilated(qa, ka, va, g) for g in range(N_DIL_GROUPS)]
    yb = _gqa(qb, kbt, vb)

    xs_mid, h2p, logits = _mix(
        x, dil_outs, yb, ga, gb, mod,
        (w_o_dil.astype(BF16), w_og, w_out.astype(BF16), row(norm2_g), router_w.astype(BF16),
         w_sh_gate.astype(BF16), w_sh_up.astype(BF16), w_sh_down.astype(BF16)))

    idx, gate, gate_tm, rank, counts = _route(logits.reshape(T, N_EXPERTS), router_bias)
    del gate
    nb_max = -(-(T * TOP_K) // BM_EXPERT) + N_EXPERTS
    nb_max = -(-nb_max // LANES) * LANES
    dest, bexp, nblk = _plan(counts, idx, rank, nb_max)

    xs = _dispatch(dest, h2p.reshape(T, HALF), nb_max * BM_EXPERT)
    ys = _experts(bexp.reshape(nb_max), nblk.reshape(LANES)[:1], xs, w_exp_gate, w_exp_up, w_exp_down, nb_max)
    out = _combine(dest, xs_mid.reshape(T, D), gate_tm, mod, row(final_g), ys, S)
    return out.reshape(B, S, D)
```

```python
import functools

import jax
import jax.numpy as jnp
from jax import lax
from jax.experimental import pallas as pl
from jax.experimental.pallas import tpu as pltpu

F32 = jnp.float32
BF16 = jnp.bfloat16
U32 = jnp.uint32
I32 = jnp.int32

D_MODEL = 1024
HEAD_DIM = 64
DIL_CONFIGS = ((128, 1), (512, 4), (2048, 16))
N_DIL_GROUPS = 3
DIL_GROUP_WIDTH = 256
DIL_WIDTH = N_DIL_GROUPS * DIL_GROUP_WIDTH
DIL_PACKED = DIL_GROUP_WIDTH // 2
DIL_HALF_WINDOW = 64
GQA_Q_WIDTH = 1024
GQA_KV_WIDTH = 256
GQA_KV_HEADS = 4
GQA_REP = 4
GRID_W = 64
ROPE_THETA = 10000.0
N_EXPERTS = 256
TOP_K = 8
N_EXPERT_GROUPS = 8
GROUP_SIZE = N_EXPERTS // N_EXPERT_GROUPS
TOPK_GROUPS = 4
EXPERT_FF = 256
ROUTED_SCALE = 2.5
NORM_EPS = 1e-6
HALF = D_MODEL // 2

LANES = 128
VMEM_LIMIT = 56 * 1024 * 1024

TM_PROJ = 256
TQ_GQA = 256
QB_DIL = 128
TR_ROUTE = 512
TM_DISPATCH = 256
TM_COMBINE = 128
BM_EXPERT = 512

NEG_INF = float("-inf")


def _params(sem):
    return pltpu.CompilerParams(dimension_semantics=sem, vmem_limit_bytes=VMEM_LIMIT)


def _const_spec(shape):
    n = len(shape)
    return pl.BlockSpec(shape, lambda *_: (0,) * n)


def _adaln_kernel(c_ref, w_ref, b_ref, o_ref):
    c = c_ref[...]
    a = c * jax.nn.sigmoid(c)
    o_ref[...] = jnp.dot(a, w_ref[...], preferred_element_type=F32,
                         precision=lax.Precision.HIGHEST) + b_ref[...]


def _adaln(c, w_ada, b_ada):
    B = c.shape[0]
    return pl.pallas_call(
        _adaln_kernel,
        grid=(6,),
        in_specs=[pl.BlockSpec((B, D_MODEL), lambda j: (0, 0)),
                  pl.BlockSpec((D_MODEL, D_MODEL), lambda j: (0, j)),
                  pl.BlockSpec((1, D_MODEL), lambda j: (0, j))],
        out_specs=pl.BlockSpec((B, D_MODEL), lambda j: (0, j)),
        out_shape=jax.ShapeDtypeStruct((B, 6 * D_MODEL), F32),
        compiler_params=_params(("arbitrary",)),
        name="adaln",
    )(c, w_ada, b_ada.reshape(1, 6 * D_MODEL))


def _tile_lanes(t, width):
    return jnp.concatenate([t] * (width // LANES), axis=1)


def _rope(v, cos, sin_signed, half):
    w = v.shape[-1]
    lane = lax.broadcasted_iota(I32, (1, w), 1)
    first = (lane % (2 * half)) < half
    up = pltpu.roll(v, w - half, axis=1)
    dn = pltpu.roll(v, half, axis=1)
    return v * _tile_lanes(cos, w) + jnp.where(first, up, dn) * _tile_lanes(sin_signed, w)


def _head_rms(v, gain, bd_ref):
    w = v.shape[-1]
    parts = []
    for c0 in range(0, w, 256):
        vc = v[:, c0:c0 + 256]
        ms = jnp.dot((vc * vc).astype(BF16), bd_ref[...], preferred_element_type=F32) * (1.0 / HEAD_DIM)
        parts.append(vc * lax.rsqrt(ms + NORM_EPS))
    return jnp.concatenate(parts, axis=1) * gain


def _in_proj_kernel(x_ref, mod_ref, g1_ref, wqa_ref, wka_ref, wva_ref, wqb_ref, wkb_ref, wvb_ref,
                    wga_ref, wgb_ref, bg_ref, qn_ref, kn_ref, cos1_ref, sin1_ref, cosb_ref, sinb_ref,
                    bd_ref, qa_ref, ka_ref, va_ref, qb_ref, kbt_ref, vb_ref, ga_ref, gb_ref):
    x = x_ref[0]
    ms = jnp.mean(x * x, axis=-1, keepdims=True)
    h = x * lax.rsqrt(ms + NORM_EPS) * g1_ref[...]
    h = h * (1.0 + mod_ref[0, 1:2, :]) + mod_ref[0, 0:1, :]
    hb = h.astype(BF16)

    def proj(w_ref):
        return jnp.dot(hb, w_ref[...], preferred_element_type=F32)

    cos1, sin1 = cos1_ref[...], sin1_ref[...]
    cosb, sinb = cosb_ref[...], sinb_ref[...]

    qa = _rope(proj(wqa_ref), cos1, sin1, HEAD_DIM // 2) * (HEAD_DIM ** -0.5)
    ka = _rope(proj(wka_ref), cos1, sin1, HEAD_DIM // 2)
    va = proj(wva_ref)
    for g in range(N_DIL_GROUPS):
        sl = slice(g * DIL_GROUP_WIDTH, (g + 1) * DIL_GROUP_WIDTH)
        qa_ref[g, 0] = _pack_halves(qa[:, sl])
        ka_ref[g, 0] = _pack_halves(ka[:, sl])
        va_ref[g, 0] = _pack_halves(va[:, sl])

    qb = _head_rms(proj(wqb_ref), qn_ref[...], bd_ref)
    qb_ref[0] = (_rope(qb, cosb, sinb, HEAD_DIM // 4) * (HEAD_DIM ** -0.5)).astype(BF16)
    kb = _rope(_head_rms(proj(wkb_ref), kn_ref[...], bd_ref), cosb, sinb, HEAD_DIM // 4)
    kbt_ref[0] = kb.T.astype(BF16)
    vb_ref[0] = proj(wvb_ref).astype(BF16)

    ga_ref[0] = jax.nn.sigmoid(proj(wga_ref) + bg_ref[:, :D_MODEL]).astype(BF16)
    gb_ref[0] = jax.nn.sigmoid(proj(wgb_ref) + bg_ref[:, D_MODEL:]).astype(BF16)


def _in_proj(x, mod, g1, weights, bgate, qn, kn, tables, bd):
    B, S, _ = x.shape
    tm = TM_PROJ
    tok = lambda w: pl.BlockSpec((1, tm, w), lambda b, i: (b, i, 0))
    tab = pl.BlockSpec((tm, LANES), lambda b, i: (i, 0))
    dil_out = pl.BlockSpec((N_DIL_GROUPS, 1, tm, DIL_PACKED), lambda b, i: (0, b, i, 0))
    dil_shape = jax.ShapeDtypeStruct((N_DIL_GROUPS, B, S, DIL_PACKED), U32)
    in_specs = ([tok(D_MODEL), pl.BlockSpec((1, 6, D_MODEL), lambda b, i: (b, 0, 0)), _const_spec((1, D_MODEL))]
                + [_const_spec(w.shape) for w in weights]
                + [_const_spec(bgate.shape), _const_spec(qn.shape), _const_spec(kn.shape), tab, tab, tab, tab,
                   _const_spec(bd.shape)])
    out_specs = [dil_out, dil_out, dil_out, tok(GQA_Q_WIDTH),
                 pl.BlockSpec((1, GQA_KV_WIDTH, tm), lambda b, i: (b, 0, i)), tok(GQA_KV_WIDTH),
                 tok(D_MODEL), tok(D_MODEL)]
    out_shape = [dil_shape, dil_shape, dil_shape,
                 jax.ShapeDtypeStruct((B, S, GQA_Q_WIDTH), BF16),
                 jax.ShapeDtypeStruct((B, GQA_KV_WIDTH, S), BF16),
                 jax.ShapeDtypeStruct((B, S, GQA_KV_WIDTH), BF16),
                 jax.ShapeDtypeStruct((B, S, D_MODEL), BF16),
                 jax.ShapeDtypeStruct((B, S, D_MODEL), BF16)]
    return pl.pallas_call(
        _in_proj_kernel,
        grid=(B, S // tm),
        in_specs=in_specs,
        out_specs=out_specs,
        out_shape=out_shape,
        compiler_params=_params(("arbitrary", "arbitrary")),
        name="in_proj",
    )(x, mod, g1, *weights, bgate, qn, kn, *tables, bd)


def _dil_kernel(q_ref, k_ref, v_ref, o_ref, lse_lo_ref, lse_hi_ref, *, dil, seq):
    n = DIL_HALF_WINDOW
    qb = min(QB_DIL, seq)
    kw = min(qb + 2 * n, seq)
    n_heads = DIL_GROUP_WIDTH // HEAD_DIM
    lane = lax.broadcasted_iota(I32, (1, DIL_GROUP_WIDTH), 1)
    head_masks = [(lane >= h * HEAD_DIM) & (lane < (h + 1) * HEAD_DIM) for h in range(n_heads)]

    def rows(ref, start, count, r):
        if dil == 1:
            return ref[pl.ds(start, count), :]
        return ref[pl.ds(start * dil + r, count, stride=dil), :]

    def load(ref, start, count, r):
        lo, hi = _unpack_halves(rows(ref, start, count, r))
        return jnp.concatenate([lo, hi], axis=1).astype(BF16)

    def block(i, r):
        a = pl.multiple_of(i * qb, qb)
        ws = pl.multiple_of(jnp.clip(a - n, 0, seq - kw), n)
        q, k, v = load(q_ref, a, qb, r), load(k_ref, ws, kw, r), load(v_ref, ws, kw, r)
        qs = jnp.concatenate([jnp.where(hm, q, jnp.zeros_like(q)) for hm in head_masks], axis=0)
        s = lax.dot_general(qs, k, (((1,), (1,)), ((), ())), preferred_element_type=F32)
        qpos = a + lax.broadcasted_iota(I32, (qb, 1), 0)
        kpos = ws + lax.broadcasted_iota(I32, (1, kw), 1)
        valid = jnp.abs(kpos - qpos) <= n
        s = jnp.where(jnp.concatenate([valid] * n_heads, axis=0), s, NEG_INF)
        m = jnp.max(s, axis=1, keepdims=True)
        p = jnp.exp(s - m)
        l = jnp.sum(p, axis=1, keepdims=True)
        pv = jnp.dot(p.astype(BF16), v, preferred_element_type=F32) * (1.0 / l)
        lse = m + jnp.log(l)
        o_acc = jnp.zeros((qb, DIL_GROUP_WIDTH), F32)
        lse_acc = jnp.zeros((qb, DIL_GROUP_WIDTH), F32)
        for h, hm in enumerate(head_masks):
            o_acc = jnp.where(hm, pv[h * qb:(h + 1) * qb], o_acc)
            lse_acc = jnp.where(hm, lse[h * qb:(h + 1) * qb], lse_acc)
        packed = _pack_halves(o_acc)
        if dil == 1:
            dst = pl.ds(a, qb)
        else:
            dst = pl.ds(a * dil + r, qb, stride=dil)
        o_ref[dst, :] = packed
        lse_lo_ref[dst, :] = lse_acc[:, :DIL_PACKED]
        lse_hi_ref[dst, :] = lse_acc[:, DIL_PACKED:]

    for r in range(dil):
        def body(i, carry, r=r):
            block(i, r)
            return carry
        lax.fori_loop(0, seq // qb, body, 0)


def _dilated(qa, ka, va, g):
    _, B, S, W = qa.shape
    dil = DIL_CONFIGS[g][1]
    in_spec = pl.BlockSpec((None, None, S, W), lambda b: (g, b, 0, 0))
    out_spec = pl.BlockSpec((None, S, W), lambda b: (b, 0, 0))
    return pl.pallas_call(
        functools.partial(_dil_kernel, dil=dil, seq=S // dil),
        grid=(B,),
        in_specs=[in_spec, in_spec, in_spec],
        out_specs=[out_spec, out_spec, out_spec],
        out_shape=[jax.ShapeDtypeStruct((B, S, W), U32), jax.ShapeDtypeStruct((B, S, W), F32),
                   jax.ShapeDtypeStruct((B, S, W), F32)],
        compiler_params=_params(("arbitrary",)),
        name=f"dilated{g}",
    )(qa, ka, va)


def _gqa_kernel(q_ref, kt_ref, v_ref, o_ref):
    tq = q_ref.shape[0]
    lane = lax.broadcasted_iota(I32, (1, GQA_KV_WIDTH), 1)
    outs = [jnp.zeros((tq, GQA_KV_WIDTH), F32) for _ in range(GQA_REP)]
    for g in range(GQA_KV_HEADS):
        hm = (lane >= g * HEAD_DIM) & (lane < (g + 1) * HEAD_DIM)
        qg = jnp.concatenate(
            [jnp.where(hm, q_ref[:, c * 256:(c + 1) * 256], jnp.zeros((tq, 256), BF16)) for c in range(GQA_REP)],
            axis=0)
        s = jnp.dot(qg, kt_ref[...], preferred_element_type=F32)
        m = jnp.max(s, axis=1, keepdims=True)
        p = jnp.exp(s - m)
        l = jnp.sum(p, axis=1, keepdims=True)
        pv = jnp.dot(p.astype(BF16), v_ref[...], preferred_element_type=F32) * (1.0 / l)
        for c in range(GQA_REP):
            outs[c] = jnp.where(hm, pv[c * tq:(c + 1) * tq], outs[c])
    for c in range(GQA_REP):
        o_ref[:, c * 256:(c + 1) * 256] = outs[c].astype(BF16)


def _gqa(qb, kbt, vb):
    B, S, _ = qb.shape
    tq = TQ_GQA
    return pl.pallas_call(
        _gqa_kernel,
        grid=(B, S // tq),
        in_specs=[pl.BlockSpec((None, tq, GQA_Q_WIDTH), lambda b, i: (b, i, 0)),
                  pl.BlockSpec((None, GQA_KV_WIDTH, S), lambda b, i: (b, 0, 0)),
                  pl.BlockSpec((None, S, GQA_KV_WIDTH), lambda b, i: (b, 0, 0))],
        out_specs=pl.BlockSpec((None, tq, GQA_Q_WIDTH), lambda b, i: (b, i, 0)),
        out_shape=jax.ShapeDtypeStruct((B, S, GQA_Q_WIDTH), BF16),
        compiler_params=_params(("arbitrary", "arbitrary")),
        name="gqa",
    )(qb, kbt, vb)


def _pack_halves(y):
    half = y.shape[1] // 2
    lo = lax.bitcast_convert_type(y[:, :half].astype(BF16).astype(F32), U32)
    hi = lax.bitcast_convert_type(y[:, half:].astype(BF16).astype(F32), U32)
    return (lo >> 16) | hi


def _unpack_halves(u):
    lo = lax.bitcast_convert_type(u << 16, F32)
    hi = lax.bitcast_convert_type(u & jnp.uint32(0xFFFF0000), F32)
    return lo, hi


def _mix_kernel(x_ref, *refs):
    dil_refs, refs = refs[:3 * N_DIL_GROUPS], refs[3 * N_DIL_GROUPS:]
    (yb_ref, ga_ref, gb_ref, mod_ref, wod_ref, wog_ref, wout_ref, g2_ref, rw_ref, wsg_ref, wsu_ref, wsd_ref,
     xs_ref, h2_ref, logit_ref) = refs

    outs = [_unpack_halves(dil_refs[3 * g][0]) for g in range(N_DIL_GROUPS)]
    halves = []
    for part in range(2):
        lses = [dil_refs[3 * g + 1 + part][0] for g in range(N_DIL_GROUPS)]
        mx = functools.reduce(jnp.maximum, lses)
        es = [jnp.exp(l - mx) for l in lses]
        num = sum(e * outs[g][part] for g, e in enumerate(es))
        halves.append(num / sum(es))
    ya = jnp.concatenate(halves, axis=1)
    y_a = jnp.dot(ya.astype(BF16), wod_ref[...], preferred_element_type=F32)
    y_b = jnp.dot(yb_ref[0], wog_ref[...], preferred_element_type=F32)
    mix = ga_ref[0].astype(F32) * y_a + gb_ref[0].astype(F32) * y_b
    mixed = jnp.dot(mix.astype(BF16), wout_ref[...], preferred_element_type=F32)
    x1 = x_ref[0] + mod_ref[0, 2:3, :] * mixed

    ms = jnp.mean(x1 * x1, axis=-1, keepdims=True)
    h2 = x1 * lax.rsqrt(ms + NORM_EPS) * g2_ref[...]
    h2 = h2 * (1.0 + mod_ref[0, 4:5, :]) + mod_ref[0, 3:4, :]
    h2b = h2.astype(BF16)
    h2_ref[0] = _pack_halves(h2)
    logit_ref[0] = jnp.dot(h2b, rw_ref[...], preferred_element_type=F32)

    gt = jnp.dot(h2b, wsg_ref[...], preferred_element_type=F32)
    up = jnp.dot(h2b, wsu_ref[...], preferred_element_type=F32)
    act = (gt * jax.nn.sigmoid(gt) * up).astype(BF16)
    shared = jnp.dot(act, wsd_ref[...], preferred_element_type=F32)
    xs_ref[0] = x1 + mod_ref[0, 5:6, :] * shared


def _mix(x, dil_outs, yb, ga, gb, mod, weights):
    B, S, _ = x.shape
    tm = TM_PROJ
    tok = lambda w: pl.BlockSpec((1, tm, w), lambda b, i: (b, i, 0))
    dil_flat = [a for group in dil_outs for a in group]
    in_specs = ([tok(D_MODEL)] + [tok(DIL_PACKED)] * len(dil_flat) + [tok(GQA_Q_WIDTH), tok(D_MODEL), tok(D_MODEL),
                pl.BlockSpec((1, 6, D_MODEL), lambda b, i: (b, 0, 0))] + [_const_spec(w.shape) for w in weights])
    return pl.pallas_call(
        _mix_kernel,
        grid=(B, S // tm),
        in_specs=in_specs,
        out_specs=[tok(D_MODEL), tok(HALF), tok(N_EXPERTS)],
        out_shape=[jax.ShapeDtypeStruct((B, S, D_MODEL), F32),
                   jax.ShapeDtypeStruct((B, S, HALF), U32),
                   jax.ShapeDtypeStruct((B, S, N_EXPERTS), F32)],
        compiler_params=_params(("arbitrary", "arbitrary")),
        name="mix",
    )(x, *dil_flat, yb, ga, gb, mod, *weights)


def _first_argmax(v, idx, big):
    m = jnp.max(v, axis=0, keepdims=True)
    i = jnp.min(jnp.where(v == m, idx, big), axis=0, keepdims=True)
    return m, i


def _route_kernel(logit_ref, bias_ref, tri_ref, idx_ref, gate_ref, gate_tm_ref, rank_ref, cnt_ref, base_ref):
    step = pl.program_id(0)
    tr = logit_ref.shape[0]

    @pl.when(step == 0)
    def _():
        base_ref[...] = jnp.zeros_like(base_ref)

    scores = jax.nn.sigmoid(logit_ref[...].T)
    biased = scores + bias_ref[...]
    row = lax.broadcasted_iota(I32, (N_EXPERTS, tr), 0)

    gscore = []
    group_row = lax.broadcasted_iota(I32, (GROUP_SIZE, tr), 0)
    for g in range(N_EXPERT_GROUPS):
        v, ri = biased[g * GROUP_SIZE:(g + 1) * GROUP_SIZE], group_row + g * GROUP_SIZE
        m1, i1 = _first_argmax(v, ri, N_EXPERTS)
        m2 = jnp.max(jnp.where(ri == i1, NEG_INF, v), axis=0, keepdims=True)
        gscore.append(m1 + m2)
    cur = jnp.concatenate(gscore, axis=0)
    gi = lax.broadcasted_iota(I32, (N_EXPERT_GROUPS, tr), 0)
    sel = jnp.zeros((N_EXPERT_GROUPS, tr), F32)
    for _ in range(TOPK_GROUPS):
        _, i = _first_argmax(cur, gi, N_EXPERT_GROUPS)
        hit = gi == i
        sel = jnp.where(hit, 1.0, sel)
        cur = jnp.where(hit, NEG_INF, cur)

    cur = jnp.concatenate(
        [jnp.where(sel[g:g + 1] > 0, biased[g * GROUP_SIZE:(g + 1) * GROUP_SIZE], NEG_INF)
         for g in range(N_EXPERT_GROUPS)], axis=0)
    idxs, gates, hits = [], [], []
    assigned = jnp.zeros((N_EXPERTS, tr), F32)
    for _ in range(TOP_K):
        _, i = _first_argmax(cur, row, N_EXPERTS)
        hit = row == i
        gates.append(jnp.sum(jnp.where(hit, scores, 0.0), axis=0, keepdims=True))
        cur = jnp.where(hit, NEG_INF, cur)
        assigned = jnp.where(hit, 1.0, assigned)
        idxs.append(i)
        hits.append(hit)
    gate = jnp.concatenate(gates, axis=0)
    gate = gate / jnp.sum(gate, axis=0, keepdims=True) * ROUTED_SCALE

    before = jnp.dot(assigned.astype(BF16), tri_ref[...], preferred_element_type=F32) + base_ref[:, 0:1]
    ranks = [jnp.sum(jnp.where(h, before, 0.0), axis=0, keepdims=True) for h in hits]

    idx_ref[...] = jnp.concatenate(idxs, axis=0)
    gate_ref[...] = gate
    rank_ref[...] = jnp.concatenate(ranks, axis=0).astype(I32)
    gate_tm_ref[...] = jnp.concatenate([gate, jnp.zeros((LANES - TOP_K, tr), F32)], axis=0).T
    base_ref[...] = base_ref[...] + jnp.sum(assigned, axis=1, keepdims=True)
    cnt_ref[...] = base_ref[...]


def _route(logits, bias):
    T = logits.shape[0]
    tr = TR_ROUTE
    tri = (lax.broadcasted_iota(I32, (tr, tr), 0) < lax.broadcasted_iota(I32, (tr, tr), 1)).astype(BF16)
    slot = pl.BlockSpec((TOP_K, tr), lambda i: (0, i))
    return pl.pallas_call(
        _route_kernel,
        grid=(T // tr,),
        in_specs=[pl.BlockSpec((tr, N_EXPERTS), lambda i: (i, 0)), _const_spec((N_EXPERTS, 1)), _const_spec((tr, tr))],
        out_specs=[slot, slot, pl.BlockSpec((tr, LANES), lambda i: (i, 0)), slot, _const_spec((N_EXPERTS, LANES))],
        out_shape=[jax.ShapeDtypeStruct((TOP_K, T), I32), jax.ShapeDtypeStruct((TOP_K, T), F32),
                   jax.ShapeDtypeStruct((T, LANES), F32), jax.ShapeDtypeStruct((TOP_K, T), I32),
                   jax.ShapeDtypeStruct((N_EXPERTS, LANES), F32)],
        scratch_shapes=[pltpu.VMEM((N_EXPERTS, LANES), F32)],
        compiler_params=_params(("arbitrary",)),
        name="route",
    )(logits, bias.reshape(N_EXPERTS, 1), tri)


def _plan_kernel(cnt_ref, idx_ref, rank_ref, ltri_ref, dest_ref, bexp_ref, nblk_ref, *, nb_max):
    tr = idx_ref.shape[1]
    blocks = jnp.floor((cnt_ref[...] + (BM_EXPERT - 1)) * (1.0 / BM_EXPERT))
    start = jnp.dot(ltri_ref[...], blocks.astype(BF16), preferred_element_type=F32)
    end = start + blocks
    blk = lax.broadcasted_iota(I32, (1, nb_max), 1).astype(F32)
    bexp = jnp.sum(jnp.where(end[:, 0:1] <= blk, 1.0, 0.0), axis=0, keepdims=True)
    bexp_ref[...] = jnp.minimum(bexp, N_EXPERTS - 1.0).astype(I32)
    nblk_ref[...] = jnp.sum(blocks, axis=0, keepdims=True).astype(I32)

    row = lax.broadcasted_iota(I32, (N_EXPERTS, tr), 0)
    start_rows = start[:, 0:1] * float(BM_EXPERT)
    dests = []
    for k in range(TOP_K):
        hit = row == idx_ref[k:k + 1, :]
        dests.append(jnp.sum(jnp.where(hit, start_rows, 0.0), axis=0, keepdims=True))
    dest_ref[...] = jnp.concatenate(dests, axis=0).astype(I32) + rank_ref[...]


def _plan(counts, idx, rank, nb_max):
    T = idx.shape[1]
    tr = TR_ROUTE
    ltri = (lax.broadcasted_iota(I32, (N_EXPERTS, N_EXPERTS), 1)
            < lax.broadcasted_iota(I32, (N_EXPERTS, N_EXPERTS), 0)).astype(BF16)
    slot = pl.BlockSpec((TOP_K, tr), lambda i: (0, i))
    return pl.pallas_call(
        functools.partial(_plan_kernel, nb_max=nb_max),
        grid=(T // tr,),
        in_specs=[_const_spec((N_EXPERTS, LANES)), slot, slot, _const_spec((N_EXPERTS, N_EXPERTS))],
        out_specs=[slot, _const_spec((1, nb_max)), _const_spec((1, LANES))],
        out_shape=[jax.ShapeDtypeStruct((TOP_K, T), I32), jax.ShapeDtypeStruct((1, nb_max), I32),
                   jax.ShapeDtypeStruct((1, LANES), I32)],
        compiler_params=_params(("arbitrary",)),
        name="plan",
    )(counts, idx, rank, ltri)


def _dispatch_kernel(dest_ref, h_ref, xs_in_ref, xs_ref, sem):
    del xs_in_ref
    tm = h_ref.shape[0]

    def row_copy(t, k):
        return pltpu.make_async_copy(h_ref.at[pl.ds(t, 1), :], xs_ref.at[pl.ds(dest_ref[k, t], 1), :], sem)

    def issue(t, carry):
        for k in range(TOP_K):
            row_copy(t, k).start()
        return carry

    lax.fori_loop(0, tm, issue, 0)
    for _ in range(TOP_K):
        pltpu.make_async_copy(h_ref, xs_ref.at[pl.ds(0, tm), :], sem).wait()


def _dispatch(dest, h2p, rows_padded):
    T = h2p.shape[0]
    tm = TM_DISPATCH
    xs0 = jnp.zeros((rows_padded, HALF), U32)
    return pl.pallas_call(
        _dispatch_kernel,
        grid=(T // tm,),
        in_specs=[pl.BlockSpec((TOP_K, tm), lambda i: (0, i), memory_space=pltpu.SMEM),
                  pl.BlockSpec((tm, HALF), lambda i: (i, 0)),
                  pl.BlockSpec(memory_space=pl.ANY)],
        out_specs=pl.BlockSpec(memory_space=pl.ANY),
        out_shape=jax.ShapeDtypeStruct((rows_padded, HALF), U32),
        scratch_shapes=[pltpu.SemaphoreType.DMA(())],
        input_output_aliases={2: 0},
        compiler_params=_params(("arbitrary",)),
        name="dispatch",
    )(dest, h2p, xs0)


def _expert_kernel(bexp_ref, nblk_ref, xs_ref, wg_ref, wu_ref, wd_ref, ys_ref, wgb_ref, wub_ref, wdb_ref):
    p = pl.program_id(0)
    prev = bexp_ref[jnp.maximum(p - 1, 0)]
    active = p < nblk_ref[0]

    @pl.when(active & ((p == 0) | (bexp_ref[p] != prev)))
    def _():
        wgb_ref[...] = wg_ref[...].astype(BF16)
        wub_ref[...] = wu_ref[...].astype(BF16)
        wdb_ref[...] = wd_ref[...].astype(BF16)

    @pl.when(active)
    def _():
        lo, hi = _unpack_halves(xs_ref[...])
        lo, hi = lo.astype(BF16), hi.astype(BF16)

        def proj(w_ref):
            return (jnp.dot(lo, w_ref[:HALF, :], preferred_element_type=F32)
                    + jnp.dot(hi, w_ref[HALF:, :], preferred_element_type=F32))

        gt, up = proj(wgb_ref), proj(wub_ref)
        act = (gt * jax.nn.sigmoid(gt) * up).astype(BF16)
        ys_ref[...] = _pack_halves(jnp.dot(act, wdb_ref[...], preferred_element_type=F32))


def _experts(bexp, nblk, xs, w_gate, w_up, w_down, nb_max):
    bm = BM_EXPERT
    rows = lambda p, be, nb: (jnp.minimum(p, nb[0] - 1), 0)
    wspec = lambda shape: pl.BlockSpec((None,) + shape, lambda p, be, nb: (be[p], 0, 0))
    grid_spec = pltpu.PrefetchScalarGridSpec(
        num_scalar_prefetch=2,
        grid=(nb_max,),
        in_specs=[pl.BlockSpec((bm, HALF), rows), wspec((D_MODEL, EXPERT_FF)), wspec((D_MODEL, EXPERT_FF)),
                  wspec((EXPERT_FF, D_MODEL))],
        out_specs=pl.BlockSpec((bm, HALF), rows),
        scratch_shapes=[pltpu.VMEM((D_MODEL, EXPERT_FF), BF16), pltpu.VMEM((D_MODEL, EXPERT_FF), BF16),
                        pltpu.VMEM((EXPERT_FF, D_MODEL), BF16)],
    )
    return pl.pallas_call(
        _expert_kernel,
        grid_spec=grid_spec,
        out_shape=jax.ShapeDtypeStruct(xs.shape, U32),
        input_output_aliases={2: 0},
        compiler_params=_params(("arbitrary",)),
        name="experts",
    )(bexp, nblk, xs, w_gate, w_up, w_down)


def _combine_kernel(dest_ref, dnext_ref, xs_ref, gate_ref, mod_ref, fg_ref, ys_ref, o_ref, buf_ref, sem):
    i = pl.program_id(0)
    n = pl.num_programs(0)
    tm = o_ref.shape[0]
    slot = i % 2

    def issue(d_ref, s):
        def body(t, carry):
            for k in range(TOP_K):
                pltpu.make_async_copy(ys_ref.at[pl.ds(d_ref[k, t], 1), :],
                                      buf_ref.at[s, k, pl.ds(t, 1), :], sem.at[s]).start()
            return carry
        lax.fori_loop(0, tm, body, 0)

    @pl.when(i == 0)
    def _():
        issue(dest_ref, 0)

    @pl.when(i + 1 < n)
    def _():
        issue(dnext_ref, 1 - slot)

    for k in range(TOP_K):
        pltpu.make_async_copy(ys_ref.at[pl.ds(0, tm), :], buf_ref.at[slot, k], sem.at[slot]).wait()

    acc_lo = jnp.zeros((tm, HALF), F32)
    acc_hi = jnp.zeros((tm, HALF), F32)
    for k in range(TOP_K):
        lo, hi = _unpack_halves(buf_ref[slot, k])
        g = gate_ref[:, k:k + 1]
        acc_lo = acc_lo + g * lo
        acc_hi = acc_hi + g * hi
    routed = jnp.concatenate([acc_lo, acc_hi], axis=1)
    x2 = xs_ref[...] + mod_ref[0, 5:6, :] * routed
    ms = jnp.mean(x2 * x2, axis=-1, keepdims=True)
    o_ref[...] = x2 * lax.rsqrt(ms + NORM_EPS) * fg_ref[...]


def _combine(dest, xs_mid, gate_tm, mod, final_g, ys, seq):
    T = xs_mid.shape[0]
    tm = TM_COMBINE
    n = T // tm
    per_seq = seq // tm
    return pl.pallas_call(
        _combine_kernel,
        grid=(n,),
        in_specs=[pl.BlockSpec((TOP_K, tm), lambda i: (0, i), memory_space=pltpu.SMEM),
                  pl.BlockSpec((TOP_K, tm), lambda i: (0, jnp.minimum(i + 1, n - 1)), memory_space=pltpu.SMEM),
                  pl.BlockSpec((tm, D_MODEL), lambda i: (i, 0)),
                  pl.BlockSpec((tm, LANES), lambda i: (i, 0)),
                  pl.BlockSpec((1, 6, D_MODEL), lambda i: (i // per_seq, 0, 0)),
                  _const_spec((1, D_MODEL)),
                  pl.BlockSpec(memory_space=pl.ANY)],
        out_specs=pl.BlockSpec((tm, D_MODEL), lambda i: (i, 0)),
        out_shape=jax.ShapeDtypeStruct((T, D_MODEL), F32),
        scratch_shapes=[pltpu.VMEM((2, TOP_K, tm, HALF), U32), pltpu.SemaphoreType.DMA((2,))],
        compiler_params=_params(("arbitrary",)),
        name="combine",
    )(dest, dest, xs_mid, gate_tm, mod, final_g, ys)


def _rope_tables(seq):
    def cos_sin(pos, dim):
        inv = ROPE_THETA ** (-jnp.arange(0, dim, 2, dtype=F32) / dim)
        ang = pos.astype(F32)[:, None] * inv[None, :]
        ang = jnp.concatenate([ang, ang], axis=-1)
        sign = jnp.where(jnp.arange(dim) < dim // 2, -1.0, 1.0).astype(F32)
        return jnp.cos(ang), jnp.sin(ang) * sign

    t = jnp.arange(seq, dtype=I32)
    cos1, sin1 = cos_sin(t, HEAD_DIM)
    cos_r, sin_r = cos_sin(t // GRID_W, HEAD_DIM // 2)
    cos_c, sin_c = cos_sin(t % GRID_W, HEAD_DIM // 2)
    cosb = jnp.concatenate([cos_r, cos_c], axis=-1)
    sinb = jnp.concatenate([sin_r, sin_c], axis=-1)
    rep = LANES // HEAD_DIM
    return tuple(jnp.tile(a, (1, rep)) for a in (cos1, sin1, cosb, sinb))


def kernel(x, c, w_ada, b_ada, norm1_g, w_in, b_gate, qn_g, kn_g, w_o_dil, w_o_gqa, w_out, norm2_g,
           router_w, router_bias, w_exp_gate, w_exp_up, w_exp_down, w_sh_gate, w_sh_up, w_sh_down, final_g):
    B, S, D = x.shape
    T = B * S
    assert D == D_MODEL and S % (TM_PROJ * 1) == 0 and S // 16 >= QB_DIL
    row = lambda v: v.reshape(1, -1)

    mod = _adaln(c, w_ada, b_ada).reshape(B, 6, D)

    wb = w_in.astype(BF16)
    offs = [0, DIL_WIDTH, 2 * DIL_WIDTH, 3 * DIL_WIDTH, 3 * DIL_WIDTH + GQA_Q_WIDTH,
            3 * DIL_WIDTH + GQA_Q_WIDTH + GQA_KV_WIDTH, 3 * DIL_WIDTH + GQA_Q_WIDTH + 2 * GQA_KV_WIDTH]
    w_qa, w_ka, w_va = (wb[:, offs[i]:offs[i + 1]] for i in range(3))
    w_qb = wb[:, offs[3]:offs[4]].reshape(D, GQA_KV_HEADS, GQA_REP, HEAD_DIM).transpose(0, 2, 1, 3).reshape(D, GQA_Q_WIDTH)
    w_kb, w_vb = wb[:, offs[4]:offs[5]], wb[:, offs[5]:offs[6]]
    w_ga, w_gb = wb[:, offs[6]:offs[6] + D], wb[:, offs[6] + D:offs[6] + 2 * D]
    w_og = w_o_gqa.reshape(GQA_KV_HEADS, GQA_REP, HEAD_DIM, D).transpose(1, 0, 2, 3).reshape(GQA_Q_WIDTH, D).astype(BF16)

    lane_head = lax.broadcasted_iota(I32, (256, 256), 0) // HEAD_DIM
    bd = (lane_head == lane_head.T).astype(BF16)
    qn = jnp.tile(row(qn_g), (1, GQA_Q_WIDTH // HEAD_DIM))
    kn = jnp.tile(row(kn_g), (1, GQA_KV_WIDTH // HEAD_DIM))

    qa, ka, va, qb, kbt, vb, ga, gb = _in_proj(
        x, mod, row(norm1_g), (w_qa, w_ka, w_va, w_qb, w_kb, w_vb, w_ga, w_gb), row(b_gate), qn, kn,
        _rope_tables(S), bd)

    dil_outs = [_dilated(qa, ka, va, g) for g in range(N_DIL_GROUPS)]
    yb = _gqa(qb, kbt, vb)

    xs_mid, h2p, logits = _mix(
        x, dil_outs, yb, ga, gb, mod,
        (w_o_dil.astype(BF16), w_og, w_out.astype(BF16), row(norm2_g), router_w.astype(BF16),
         w_sh_gate.astype(BF16), w_sh_up.astype(BF16), w_sh_down.astype(BF16)))

    idx, gate, gate_tm, rank, counts = _route(logits.reshape(T, N_EXPERTS), router_bias)
    del gate
    nb_max = -(-(T * TOP_K) // BM_EXPERT) + N_EXPERTS
    nb_max = -(-nb_max // LANES) * LANES
    dest, bexp, nblk = _plan(counts, idx, rank, nb_max)

    xs = _dispatch(dest, h2p.reshape(T, HALF), nb_max * BM_EXPERT)
    ys = _experts(bexp.reshape(nb_max), nblk.reshape(LANES)[:1], xs, w_exp_gate, w_exp_up, w_exp_down, nb_max)
    out = _combine(dest, xs_mid.reshape(T, D), gate_tm, mod, row(final_g), ys, S)
    return out.reshape(B, S, D)
```

```python
import functools

import jax
import jax.numpy as jnp
from jax import lax
from jax.experimental import pallas as pl
from jax.experimental.pallas import tpu as pltpu

F32 = jnp.float32
BF16 = jnp.bfloat16
U32 = jnp.uint32
I32 = jnp.int32

D_MODEL = 1024
HEAD_DIM = 64
DIL_CONFIGS = ((128, 1), (512, 4), (2048, 16))
N_DIL_GROUPS = 3
DIL_GROUP_WIDTH = 256
DIL_WIDTH = N_DIL_GROUPS * DIL_GROUP_WIDTH
DIL_PACKED = DIL_GROUP_WIDTH // 2
DIL_HALF_WINDOW = 64
GQA_Q_WIDTH = 1024
GQA_KV_WIDTH = 256
GQA_KV_HEADS = 4
GQA_REP = 4
GRID_W = 64
ROPE_THETA = 10000.0
N_EXPERTS = 256
TOP_K = 8
N_EXPERT_GROUPS = 8
GROUP_SIZE = N_EXPERTS // N_EXPERT_GROUPS
TOPK_GROUPS = 4
EXPERT_FF = 256
ROUTED_SCALE = 2.5
NORM_EPS = 1e-6
HALF = D_MODEL // 2

LANES = 128
VMEM_LIMIT = 56 * 1024 * 1024

TM_PROJ = 256
TQ_GQA = 256
QB_DIL = 128
TR_ROUTE = 512
TM_DISPATCH = 256
TM_COMBINE = 128
BM_EXPERT = 512

NEG_INF = float("-inf")


def _params(sem):
    return pltpu.CompilerParams(dimension_semantics=sem, vmem_limit_bytes=VMEM_LIMIT)


def _const_spec(shape):
    n = len(shape)
    return pl.BlockSpec(shape, lambda *_: (0,) * n)


def _adaln_kernel(c_ref, w_ref, b_ref, o_ref):
    c = c_ref[...]
    a = c * jax.nn.sigmoid(c)
    o_ref[...] = jnp.dot(a, w_ref[...], preferred_element_type=F32,
                         precision=lax.Precision.HIGHEST) + b_ref[...]


def _adaln(c, w_ada, b_ada):
    B = c.shape[0]
    return pl.pallas_call(
        _adaln_kernel,
        grid=(6,),
        in_specs=[pl.BlockSpec((B, D_MODEL), lambda j: (0, 0)),
                  pl.BlockSpec((D_MODEL, D_MODEL), lambda j: (0, j)),
                  pl.BlockSpec((1, D_MODEL), lambda j: (0, j))],
        out_specs=pl.BlockSpec((B, D_MODEL), lambda j: (0, j)),
        out_shape=jax.ShapeDtypeStruct((B, 6 * D_MODEL), F32),
        compiler_params=_params(("arbitrary",)),
        name="adaln",
    )(c, w_ada, b_ada.reshape(1, 6 * D_MODEL))


def _tile_lanes(t, width):
    return jnp.concatenate([t] * (width // LANES), axis=1)


def _rope(v, cos, sin_signed, half):
    w = v.shape[-1]
    lane = lax.broadcasted_iota(I32, (1, w), 1)
    first = (lane % (2 * half)) < half
    up = pltpu.roll(v, w - half, axis=1)
    dn = pltpu.roll(v, half, axis=1)
    return v * _tile_lanes(cos, w) + jnp.where(first, up, dn) * _tile_lanes(sin_signed, w)


def _head_rms(v, gain, bd_ref):
    w = v.shape[-1]
    parts = []
    for c0 in range(0, w, 256):
        vc = v[:, c0:c0 + 256]
        ms = jnp.dot((vc * vc).astype(BF16), bd_ref[...], preferred_element_type=F32) * (1.0 / HEAD_DIM)
        parts.append(vc * lax.rsqrt(ms + NORM_EPS))
    return jnp.concatenate(parts, axis=1) * gain


def _in_proj_kernel(x_ref, mod_ref, g1_ref, wqa_ref, wka_ref, wva_ref, wqb_ref, wkb_ref, wvb_ref,
                    wga_ref, wgb_ref, bg_ref, qn_ref, kn_ref, cos1_ref, sin1_ref, cosb_ref, sinb_ref,
                    bd_ref, qa_ref, ka_ref, va_ref, qb_ref, kbt_ref, vb_ref, ga_ref, gb_ref):
    x = x_ref[0]
    ms = jnp.mean(x * x, axis=-1, keepdims=True)
    h = x * lax.rsqrt(ms + NORM_EPS) * g1_ref[...]
    h = h * (1.0 + mod_ref[0, 1:2, :]) + mod_ref[0, 0:1, :]
    hb = h.astype(BF16)

    def proj(w_ref):
        return jnp.dot(hb, w_ref[...], preferred_element_type=F32)

    cos1, sin1 = cos1_ref[...], sin1_ref[...]
    cosb, sinb = cosb_ref[...], sinb_ref[...]

    qa = _rope(proj(wqa_ref), cos1, sin1, HEAD_DIM // 2) * (HEAD_DIM ** -0.5)
    ka = _rope(proj(wka_ref), cos1, sin1, HEAD_DIM // 2)
    va = proj(wva_ref)
    for g in range(N_DIL_GROUPS):
        sl = slice(g * DIL_GROUP_WIDTH, (g + 1) * DIL_GROUP_WIDTH)
        qa_ref[g, 0] = _pack_halves(qa[:, sl])
        ka_ref[g, 0] = _pack_halves(ka[:, sl])
        va_ref[g, 0] = _pack_halves(va[:, sl])

    qb = _head_rms(proj(wqb_ref), qn_ref[...], bd_ref)
    qb_ref[0] = (_rope(qb, cosb, sinb, HEAD_DIM // 4) * (HEAD_DIM ** -0.5)).astype(BF16)
    kb = _rope(_head_rms(proj(wkb_ref), kn_ref[...], bd_ref), cosb, sinb, HEAD_DIM // 4)
    kbt_ref[0] = kb.T.astype(BF16)
    vb_ref[0] = proj(wvb_ref).astype(BF16)

    ga_ref[0] = jax.nn.sigmoid(proj(wga_ref) + bg_ref[:, :D_MODEL]).astype(BF16)
    gb_ref[0] = jax.nn.sigmoid(proj(wgb_ref) + bg_ref[:, D_MODEL:]).astype(BF16)


def _in_proj(x, mod, g1, weights, bgate, qn, kn, tables, bd):
    B, S, _ = x.shape
    tm = TM_PROJ
    tok = lambda w: pl.BlockSpec((1, tm, w), lambda b, i: (b, i, 0))
    tab = pl.BlockSpec((tm, LANES), lambda b, i: (i, 0))
    dil_out = pl.BlockSpec((N_DIL_GROUPS, 1, tm, DIL_PACKED), lambda b, i: (0, b, i, 0))
    dil_shape = jax.ShapeDtypeStruct((N_DIL_GROUPS, B, S, DIL_PACKED), U32)
    in_specs = ([tok(D_MODEL), pl.BlockSpec((1, 6, D_MODEL), lambda b, i: (b, 0, 0)), _const_spec((1, D_MODEL))]
                + [_const_spec(w.shape) for w in weights]
                + [_const_spec(bgate.shape), _const_spec(qn.shape), _const_spec(kn.shape), tab, tab, tab, tab,
                   _const_spec(bd.shape)])
    out_specs = [dil_out, dil_out, dil_out, tok(GQA_Q_WIDTH),
                 pl.BlockSpec((1, GQA_KV_WIDTH, tm), lambda b, i: (b, 0, i)), tok(GQA_KV_WIDTH),
                 tok(D_MODEL), tok(D_MODEL)]
    out_shape = [dil_shape, dil_shape, dil_shape,
                 jax.ShapeDtypeStruct((B, S, GQA_Q_WIDTH), BF16),
                 jax.ShapeDtypeStruct((B, GQA_KV_WIDTH, S), BF16),
                 jax.ShapeDtypeStruct((B, S, GQA_KV_WIDTH), BF16),
                 jax.ShapeDtypeStruct((B, S, D_MODEL), BF16),
                 jax.ShapeDtypeStruct((B, S, D_MODEL), BF16)]
    return pl.pallas_call(
        _in_proj_kernel,
        grid=(B, S // tm),
        in_specs=in_specs,
        out_specs=out_specs,
        out_shape=out_shape,
        compiler_params=_params(("arbitrary", "arbitrary")),
        name="in_proj",
    )(x, mod, g1, *weights, bgate, qn, kn, *tables, bd)


def _dil_kernel(q_ref, k_ref, v_ref, o_ref, lse_lo_ref, lse_hi_ref, *, dil, seq):
    n = DIL_HALF_WINDOW
    qb = min(QB_DIL, seq)
    kw = min(qb + 2 * n, seq)
    n_heads = DIL_GROUP_WIDTH // HEAD_DIM
    lane = lax.broadcasted_iota(I32, (1, DIL_GROUP_WIDTH), 1)
    head_masks = [(lane >= h * HEAD_DIM) & (lane < (h + 1) * HEAD_DIM) for h in range(n_heads)]

    def rows(ref, start, count, r):
        if dil == 1:
            return ref[pl.ds(start, count), :]
        return ref[pl.ds(start * dil + r, count, stride=dil), :]

    def load(ref, start, count, r):
        lo, hi = _unpack_halves(rows(ref, start, count, r))
        return jnp.concatenate([lo, hi], axis=1).astype(BF16)

    def block(i, r):
        a = pl.multiple_of(i * qb, qb)
        ws = pl.multiple_of(jnp.clip(a - n, 0, seq - kw), n)
        q, k, v = load(q_ref, a, qb, r), load(k_ref, ws, kw, r), load(v_ref, ws, kw, r)
        qs = jnp.concatenate([jnp.where(hm, q, jnp.zeros_like(q)) for hm in head_masks], axis=0)
        s = lax.dot_general(qs, k, (((1,), (1,)), ((), ())), preferred_element_type=F32)
        qpos = a + lax.broadcasted_iota(I32, (qb, 1), 0)
        kpos = ws + lax.broadcasted_iota(I32, (1, kw), 1)
        valid = jnp.abs(kpos - qpos) <= n
        s = jnp.where(jnp.concatenate([valid] * n_heads, axis=0), s, NEG_INF)
        m = jnp.max(s, axis=1, keepdims=True)
        p = jnp.exp(s - m)
        l = jnp.sum(p, axis=1, keepdims=True)
        pv = jnp.dot(p.astype(BF16), v, preferred_element_type=F32) * (1.0 / l)
        lse = m + jnp.log(l)
        o_acc = jnp.zeros((qb, DIL_GROUP_WIDTH), F32)
        lse_acc = jnp.zeros((qb, DIL_GROUP_WIDTH), F32)
        for h, hm in enumerate(head_masks):
            o_acc = jnp.where(hm, pv[h * qb:(h + 1) * qb], o_acc)
            lse_acc = jnp.where(hm, lse[h * qb:(h + 1) * qb], lse_acc)
        packed = _pack_halves(o_acc)
        if dil == 1:
            dst = pl.ds(a, qb)
        else:
            dst = pl.ds(a * dil + r, qb, stride=dil)
        o_ref[dst, :] = packed
        lse_lo_ref[dst, :] = lse_acc[:, :DIL_PACKED]
        lse_hi_ref[dst, :] = lse_acc[:, DIL_PACKED:]

    for r in range(dil):
        def body(i, carry, r=r):
            block(i, r)
            return carry
        lax.fori_loop(0, seq // qb, body, 0)


def _dilated(qa, ka, va, g):
    _, B, S, W = qa.shape
    dil = DIL_CONFIGS[g][1]
    in_spec = pl.BlockSpec((None, None, S, W), lambda b: (g, b, 0, 0))
    out_spec = pl.BlockSpec((None, S, W), lambda b: (b, 0, 0))
    return pl.pallas_call(
        functools.partial(_dil_kernel, dil=dil, seq=S // dil),
        grid=(B,),
        in_specs=[in_spec, in_spec, in_spec],
        out_specs=[out_spec, out_spec, out_spec],
        out_shape=[jax.ShapeDtypeStruct((B, S, W), U32), jax.ShapeDtypeStruct((B, S, W), F32),
                   jax.ShapeDtypeStruct((B, S, W), F32)],
        compiler_params=_params(("arbitrary",)),
        name=f"dilated{g}",
    )(qa, ka, va)


def _gqa_kernel(q_ref, kt_ref, v_ref, o_ref, zero_ref):
    zero_ref[...] = jnp.zeros_like(zero_ref)
    tq = q_ref.shape[0]
    lane = lax.broadcasted_iota(I32, (1, GQA_KV_WIDTH), 1)
    outs = [jnp.zeros((tq, GQA_KV_WIDTH), F32) for _ in range(GQA_REP)]
    for g in range(GQA_KV_HEADS):
        hm = (lane >= g * HEAD_DIM) & (lane < (g + 1) * HEAD_DIM)
        qg = jnp.concatenate(
            [jnp.where(hm, q_ref[:, c * 256:(c + 1) * 256], jnp.zeros((tq, 256), BF16)) for c in range(GQA_REP)],
            axis=0)
        s = jnp.dot(qg, kt_ref[...], preferred_element_type=F32)
        m = jnp.max(s, axis=1, keepdims=True)
        p = jnp.exp(s - m)
        l = jnp.sum(p, axis=1, keepdims=True)
        pv = jnp.dot(p.astype(BF16), v_ref[...], preferred_element_type=F32) * (1.0 / l)
        for c in range(GQA_REP):
            outs[c] = jnp.where(hm, pv[c * tq:(c + 1) * tq], outs[c])
    for c in range(GQA_REP):
        o_ref[:, c * 256:(c + 1) * 256] = outs[c].astype(BF16)


def _gqa(qb, kbt, vb, rows_padded):
    B, S, _ = qb.shape
    tq = TQ_GQA
    steps = B * (S // tq)
    assert rows_padded % steps == 0
    zrows = rows_padded // steps
    return pl.pallas_call(
        _gqa_kernel,
        grid=(B, S // tq),
        in_specs=[pl.BlockSpec((None, tq, GQA_Q_WIDTH), lambda b, i: (b, i, 0)),
                  pl.BlockSpec((None, GQA_KV_WIDTH, S), lambda b, i: (b, 0, 0)),
                  pl.BlockSpec((None, S, GQA_KV_WIDTH), lambda b, i: (b, 0, 0))],
        out_specs=[pl.BlockSpec((None, tq, GQA_Q_WIDTH), lambda b, i: (b, i, 0)),
                   pl.BlockSpec((zrows * ROW_SUBLANES, LANES), lambda b, i: (b * (S // tq) + i, 0))],
        out_shape=[jax.ShapeDtypeStruct((B, S, GQA_Q_WIDTH), BF16),
                   jax.ShapeDtypeStruct((rows_padded * ROW_SUBLANES, LANES), U32)],
        compiler_params=_params(("arbitrary", "arbitrary")),
        name="gqa",
    )(qb, kbt, vb)


def _pack_halves(y):
    half = y.shape[1] // 2
    lo = lax.bitcast_convert_type(y[:, :half].astype(BF16).astype(F32), U32)
    hi = lax.bitcast_convert_type(y[:, half:].astype(BF16).astype(F32), U32)
    return (lo >> 16) | hi


def _unpack_halves(u):
    lo = lax.bitcast_convert_type(u << 16, F32)
    hi = lax.bitcast_convert_type(u & jnp.uint32(0xFFFF0000), F32)
    return lo, hi


ROW_SUBLANES = HALF // LANES


def _store_rows(ref, packed):
    m = packed.shape[0]
    for c in range(ROW_SUBLANES):
        ref[pl.ds(c, m, stride=ROW_SUBLANES), :] = packed[:, c * LANES:(c + 1) * LANES]


def _load_rows(ref, m):
    return jnp.concatenate([ref[pl.ds(c, m, stride=ROW_SUBLANES), :] for c in range(ROW_SUBLANES)], axis=1)


def _mix_kernel(x_ref, *refs):
    dil_refs, refs = refs[:3 * N_DIL_GROUPS], refs[3 * N_DIL_GROUPS:]
    (yb_ref, ga_ref, gb_ref, mod_ref, wod_ref, wog_ref, wout_ref, g2_ref, rw_ref, wsg_ref, wsu_ref, wsd_ref,
     xs_ref, h2_ref, logit_ref) = refs

    outs = [_unpack_halves(dil_refs[3 * g][0]) for g in range(N_DIL_GROUPS)]
    halves = []
    for part in range(2):
        lses = [dil_refs[3 * g + 1 + part][0] for g in range(N_DIL_GROUPS)]
        mx = functools.reduce(jnp.maximum, lses)
        es = [jnp.exp(l - mx) for l in lses]
        num = sum(e * outs[g][part] for g, e in enumerate(es))
        halves.append(num / sum(es))
    ya = jnp.concatenate(halves, axis=1)
    y_a = jnp.dot(ya.astype(BF16), wod_ref[...], preferred_element_type=F32)
    y_b = jnp.dot(yb_ref[0], wog_ref[...], preferred_element_type=F32)
    mix = ga_ref[0].astype(F32) * y_a + gb_ref[0].astype(F32) * y_b
    mixed = jnp.dot(mix.astype(BF16), wout_ref[...], preferred_element_type=F32)
    x1 = x_ref[0] + mod_ref[0, 2:3, :] * mixed

    ms = jnp.mean(x1 * x1, axis=-1, keepdims=True)
    h2 = x1 * lax.rsqrt(ms + NORM_EPS) * g2_ref[...]
    h2 = h2 * (1.0 + mod_ref[0, 4:5, :]) + mod_ref[0, 3:4, :]
    h2b = h2.astype(BF16)
    _store_rows(h2_ref, _pack_halves(h2))
    logit_ref[0] = jnp.dot(h2b, rw_ref[...], preferred_element_type=F32)

    gt = jnp.dot(h2b, wsg_ref[...], preferred_element_type=F32)
    up = jnp.dot(h2b, wsu_ref[...], preferred_element_type=F32)
    act = (gt * jax.nn.sigmoid(gt) * up).astype(BF16)
    shared = jnp.dot(act, wsd_ref[...], preferred_element_type=F32)
    xs_ref[0] = x1 + mod_ref[0, 5:6, :] * shared


def _mix(x, dil_outs, yb, ga, gb, mod, weights):
    B, S, _ = x.shape
    tm = TM_PROJ
    tok = lambda w: pl.BlockSpec((1, tm, w), lambda b, i: (b, i, 0))
    dil_flat = [a for group in dil_outs for a in group]
    in_specs = ([tok(D_MODEL)] + [tok(DIL_PACKED)] * len(dil_flat) + [tok(GQA_Q_WIDTH), tok(D_MODEL), tok(D_MODEL),
                pl.BlockSpec((1, 6, D_MODEL), lambda b, i: (b, 0, 0))] + [_const_spec(w.shape) for w in weights])
    return pl.pallas_call(
        _mix_kernel,
        grid=(B, S // tm),
        in_specs=in_specs,
        out_specs=[tok(D_MODEL),
                   pl.BlockSpec((tm * ROW_SUBLANES, LANES), lambda b, i: (b * (S // tm) + i, 0)),
                   tok(N_EXPERTS)],
        out_shape=[jax.ShapeDtypeStruct((B, S, D_MODEL), F32),
                   jax.ShapeDtypeStruct((B * S * ROW_SUBLANES, LANES), U32),
                   jax.ShapeDtypeStruct((B, S, N_EXPERTS), F32)],
        compiler_params=_params(("arbitrary", "arbitrary")),
        name="mix",
    )(x, *dil_flat, yb, ga, gb, mod, *weights)


def _first_argmax(v, idx, big):
    m = jnp.max(v, axis=0, keepdims=True)
    i = jnp.min(jnp.where(v == m, idx, big), axis=0, keepdims=True)
    return m, i


def _route_kernel(logit_ref, bias_ref, tri_ref, idx_ref, gate_ref, gate_tm_ref, rank_ref, cnt_ref, base_ref):
    step = pl.program_id(0)
    tr = logit_ref.shape[0]

    @pl.when(step == 0)
    def _():
        base_ref[...] = jnp.zeros_like(base_ref)

    scores = jax.nn.sigmoid(logit_ref[...].T)
    biased = scores + bias_ref[...]
    row = lax.broadcasted_iota(I32, (N_EXPERTS, tr), 0)

    gscore = []
    group_row = lax.broadcasted_iota(I32, (GROUP_SIZE, tr), 0)
    for g in range(N_EXPERT_GROUPS):
        v, ri = biased[g * GROUP_SIZE:(g + 1) * GROUP_SIZE], group_row + g * GROUP_SIZE
        m1, i1 = _first_argmax(v, ri, N_EXPERTS)
        m2 = jnp.max(jnp.where(ri == i1, NEG_INF, v), axis=0, keepdims=True)
        gscore.append(m1 + m2)
    cur = jnp.concatenate(gscore, axis=0)
    gi = lax.broadcasted_iota(I32, (N_EXPERT_GROUPS, tr), 0)
    sel = jnp.zeros((N_EXPERT_GROUPS, tr), F32)
    for _ in range(TOPK_GROUPS):
        _, i = _first_argmax(cur, gi, N_EXPERT_GROUPS)
        hit = gi == i
        sel = jnp.where(hit, 1.0, sel)
        cur = jnp.where(hit, NEG_INF, cur)

    cur = jnp.concatenate(
        [jnp.where(sel[g:g + 1] > 0, biased[g * GROUP_SIZE:(g + 1) * GROUP_SIZE], NEG_INF)
         for g in range(N_EXPERT_GROUPS)], axis=0)
    idxs, gates, hits = [], [], []
    assigned = jnp.zeros((N_EXPERTS, tr), F32)
    for _ in range(TOP_K):
        _, i = _first_argmax(cur, row, N_EXPERTS)
        hit = row == i
        gates.append(jnp.sum(jnp.where(hit, scores, 0.0), axis=0, keepdims=True))
        cur = jnp.where(hit, NEG_INF, cur)
        assigned = jnp.where(hit, 1.0, assigned)
        idxs.append(i)
        hits.append(hit)
    gate = jnp.concatenate(gates, axis=0)
    gate = gate / jnp.sum(gate, axis=0, keepdims=True) * ROUTED_SCALE

    before = jnp.dot(assigned.astype(BF16), tri_ref[...], preferred_element_type=F32) + base_ref[:, 0:1]
    ranks = [jnp.sum(jnp.where(h, before, 0.0), axis=0, keepdims=True) for h in hits]

    idx_ref[...] = jnp.concatenate(idxs, axis=0)
    gate_ref[...] = gate
    rank_ref[...] = jnp.concatenate(ranks, axis=0).astype(I32)
    gate_tm_ref[...] = jnp.concatenate([gate, jnp.zeros((LANES - TOP_K, tr), F32)], axis=0).T
    base_ref[...] = base_ref[...] + jnp.sum(assigned, axis=1, keepdims=True)
    cnt_ref[...] = base_ref[...]


def _route(logits, bias):
    T = logits.shape[0]
    tr = TR_ROUTE
    tri = (lax.broadcasted_iota(I32, (tr, tr), 0) < lax.broadcasted_iota(I32, (tr, tr), 1)).astype(BF16)
    slot = pl.BlockSpec((TOP_K, tr), lambda i: (0, i))
    return pl.pallas_call(
        _route_kernel,
        grid=(T // tr,),
        in_specs=[pl.BlockSpec((tr, N_EXPERTS), lambda i: (i, 0)), _const_spec((N_EXPERTS, 1)), _const_spec((tr, tr))],
        out_specs=[slot, slot, pl.BlockSpec((tr, LANES), lambda i: (i, 0)), slot, _const_spec((N_EXPERTS, LANES))],
        out_shape=[jax.ShapeDtypeStruct((TOP_K, T), I32), jax.ShapeDtypeStruct((TOP_K, T), F32),
                   jax.ShapeDtypeStruct((T, LANES), F32), jax.ShapeDtypeStruct((TOP_K, T), I32),
                   jax.ShapeDtypeStruct((N_EXPERTS, LANES), F32)],
        scratch_shapes=[pltpu.VMEM((N_EXPERTS, LANES), F32)],
        compiler_params=_params(("arbitrary",)),
        name="route",
    )(logits, bias.reshape(N_EXPERTS, 1), tri)


def _plan_kernel(cnt_ref, idx_ref, rank_ref, ltri_ref, dest_ref, bexp_ref, nblk_ref, *, nb_max):
    tr = idx_ref.shape[1]
    blocks = jnp.floor((cnt_ref[...] + (BM_EXPERT - 1)) * (1.0 / BM_EXPERT))
    start = jnp.dot(ltri_ref[...], blocks.astype(BF16), preferred_element_type=F32)
    end = start + blocks
    blk = lax.broadcasted_iota(I32, (1, nb_max), 1).astype(F32)
    bexp = jnp.sum(jnp.where(end[:, 0:1] <= blk, 1.0, 0.0), axis=0, keepdims=True)
    bexp_ref[...] = jnp.minimum(bexp, N_EXPERTS - 1.0).astype(I32)
    nblk_ref[...] = jnp.sum(blocks, axis=0, keepdims=True).astype(I32)

    row = lax.broadcasted_iota(I32, (N_EXPERTS, tr), 0)
    start_rows = start[:, 0:1] * float(BM_EXPERT)
    dests = []
    for k in range(TOP_K):
        hit = row == idx_ref[k:k + 1, :]
        dests.append(jnp.sum(jnp.where(hit, start_rows, 0.0), axis=0, keepdims=True))
    dest = jnp.concatenate(dests, axis=0) + rank_ref[...].astype(F32)
    dest_ref[...] = jnp.concatenate([dest, jnp.zeros((LANES - TOP_K, tr), F32)], axis=0).T.astype(I32)


def _plan(counts, idx, rank, nb_max):
    T = idx.shape[1]
    tr = TR_ROUTE
    ltri = (lax.broadcasted_iota(I32, (N_EXPERTS, N_EXPERTS), 1)
            < lax.broadcasted_iota(I32, (N_EXPERTS, N_EXPERTS), 0)).astype(BF16)
    slot = pl.BlockSpec((TOP_K, tr), lambda i: (0, i))
    return pl.pallas_call(
        functools.partial(_plan_kernel, nb_max=nb_max),
        grid=(T // tr,),
        in_specs=[_const_spec((N_EXPERTS, LANES)), slot, slot, _const_spec((N_EXPERTS, N_EXPERTS))],
        out_specs=[pl.BlockSpec((tr, LANES), lambda i: (i, 0)), _const_spec((1, nb_max)), _const_spec((1, LANES))],
        out_shape=[jax.ShapeDtypeStruct((T, LANES), I32), jax.ShapeDtypeStruct((1, nb_max), I32),
                   jax.ShapeDtypeStruct((1, LANES), I32)],
        compiler_params=_params(("arbitrary",)),
        name="plan",
    )(counts, idx, rank, ltri)


def _dispatch_kernel(dest_ref, h_ref, xs_in_ref, xs_ref, sem):
    del xs_in_ref
    tm = h_ref.shape[0]

    def issue(t, carry):
        for k in range(TOP_K):
            pltpu.make_async_copy(h_ref.at[t], xs_ref.at[dest_ref[t * TOP_K + k]], sem).start(priority=k % 2)
        return carry

    lax.fori_loop(0, tm, issue, 0)
    for _ in range(TOP_K):
        pltpu.make_async_copy(h_ref, xs_ref.at[pl.ds(0, tm)], sem).wait()


def _dispatch(dest_flat, h_rows, xs0):
    T = h_rows.shape[0]
    rows_padded = xs0.shape[0]
    tm = TM_DISPATCH
    return pl.pallas_call(
        _dispatch_kernel,
        grid=(T // tm,),
        in_specs=[pl.BlockSpec((tm * TOP_K,), lambda i: (i,), memory_space=pltpu.SMEM),
                  pl.BlockSpec((tm, ROW_SUBLANES, LANES), lambda i: (i, 0, 0)),
                  pl.BlockSpec(memory_space=pl.ANY)],
        out_specs=pl.BlockSpec(memory_space=pl.ANY),
        out_shape=jax.ShapeDtypeStruct((rows_padded, ROW_SUBLANES, LANES), U32),
        scratch_shapes=[pltpu.SemaphoreType.DMA(())],
        input_output_aliases={2: 0},
        compiler_params=_params(("arbitrary",)),
        name="dispatch",
    )(dest_flat, h_rows, xs0)


def _expert_kernel(bexp_ref, nblk_ref, xs_ref, wg_ref, wu_ref, wd_ref, ys_ref, wgb_ref, wub_ref, wdb_ref):
    p = pl.program_id(0)
    prev = bexp_ref[jnp.maximum(p - 1, 0)]
    active = p < nblk_ref[0]

    @pl.when(active & ((p == 0) | (bexp_ref[p] != prev)))
    def _():
        wgb_ref[...] = wg_ref[...].astype(BF16)
        wub_ref[...] = wu_ref[...].astype(BF16)
        wdb_ref[...] = wd_ref[...].astype(BF16)

    @pl.when(active)
    def _():
        lo, hi = _unpack_halves(_load_rows(xs_ref, BM_EXPERT))
        lo, hi = lo.astype(BF16), hi.astype(BF16)

        def proj(w_ref):
            return (jnp.dot(lo, w_ref[:HALF, :], preferred_element_type=F32)
                    + jnp.dot(hi, w_ref[HALF:, :], preferred_element_type=F32))

        gt, up = proj(wgb_ref), proj(wub_ref)
        act = (gt * jax.nn.sigmoid(gt) * up).astype(BF16)
        _store_rows(ys_ref, _pack_halves(jnp.dot(act, wdb_ref[...], preferred_element_type=F32)))


def _experts(bexp, nblk, xs, w_gate, w_up, w_down, nb_max):
    bm = BM_EXPERT
    rows = lambda p, be, nb: (jnp.minimum(p, nb[0] - 1), 0)
    wspec = lambda shape: pl.BlockSpec((None,) + shape, lambda p, be, nb: (be[p], 0, 0))
    grid_spec = pltpu.PrefetchScalarGridSpec(
        num_scalar_prefetch=2,
        grid=(nb_max,),
        in_specs=[pl.BlockSpec((bm * ROW_SUBLANES, LANES), rows), wspec((D_MODEL, EXPERT_FF)),
                  wspec((D_MODEL, EXPERT_FF)), wspec((EXPERT_FF, D_MODEL))],
        out_specs=pl.BlockSpec((bm * ROW_SUBLANES, LANES), rows),
        scratch_shapes=[pltpu.VMEM((D_MODEL, EXPERT_FF), BF16), pltpu.VMEM((D_MODEL, EXPERT_FF), BF16),
                        pltpu.VMEM((EXPERT_FF, D_MODEL), BF16)],
    )
    return pl.pallas_call(
        _expert_kernel,
        grid_spec=grid_spec,
        out_shape=jax.ShapeDtypeStruct(xs.shape, U32),
        input_output_aliases={2: 0},
        compiler_params=_params(("arbitrary",)),
        name="experts",
    )(bexp, nblk, xs, w_gate, w_up, w_down)


def _combine_kernel(dest_ref, dnext_ref, xs_ref, gate_ref, mod_ref, fg_ref, ys_ref, ys2d_ref, o_ref, buf_ref, sem):
    i = pl.program_id(0)
    n = pl.num_programs(0)
    tm = o_ref.shape[0]
    slot = i % 2

    def issue(d_ref, s):
        def body(t, carry):
            for k in range(TOP_K):
                pltpu.make_async_copy(ys_ref.at[d_ref[t * TOP_K + k]],
                                      buf_ref.at[s, k, pl.ds(t * ROW_SUBLANES, ROW_SUBLANES), :],
                                      sem.at[s]).start(priority=k % 2)
            return carry
        lax.fori_loop(0, tm, body, 0)

    @pl.when(i == 0)
    def _():
        issue(dest_ref, 0)

    @pl.when(i + 1 < n)
    def _():
        issue(dnext_ref, 1 - slot)

    for k in range(TOP_K):
        pltpu.make_async_copy(ys2d_ref.at[pl.ds(0, tm * ROW_SUBLANES), :], buf_ref.at[slot, k], sem.at[slot]).wait()

    acc_lo = jnp.zeros((tm, HALF), F32)
    acc_hi = jnp.zeros((tm, HALF), F32)
    for k in range(TOP_K):
        lo, hi = _unpack_halves(_load_rows(buf_ref.at[slot, k], tm))
        g = gate_ref[:, k:k + 1]
        acc_lo = acc_lo + g * lo
        acc_hi = acc_hi + g * hi
    routed = jnp.concatenate([acc_lo, acc_hi], axis=1)
    x2 = xs_ref[...] + mod_ref[0, 5:6, :] * routed
    ms = jnp.mean(x2 * x2, axis=-1, keepdims=True)
    o_ref[...] = x2 * lax.rsqrt(ms + NORM_EPS) * fg_ref[...]


def _combine(dest_flat, xs_mid, gate_tm, mod, final_g, ys2d, seq):
    T = xs_mid.shape[0]
    tm = TM_COMBINE
    n = T // tm
    per_seq = seq // tm
    ys_rows = ys2d.reshape(-1, ROW_SUBLANES, LANES)
    return pl.pallas_call(
        _combine_kernel,
        grid=(n,),
        in_specs=[pl.BlockSpec((tm * TOP_K,), lambda i: (i,), memory_space=pltpu.SMEM),
                  pl.BlockSpec((tm * TOP_K,), lambda i: (jnp.minimum(i + 1, n - 1),), memory_space=pltpu.SMEM),
                  pl.BlockSpec((tm, D_MODEL), lambda i: (i, 0)),
                  pl.BlockSpec((tm, LANES), lambda i: (i, 0)),
                  pl.BlockSpec((1, 6, D_MODEL), lambda i: (i // per_seq, 0, 0)),
                  _const_spec((1, D_MODEL)),
                  pl.BlockSpec(memory_space=pl.ANY),
                  pl.BlockSpec(memory_space=pl.ANY)],
        out_specs=pl.BlockSpec((tm, D_MODEL), lambda i: (i, 0)),
        out_shape=jax.ShapeDtypeStruct((T, D_MODEL), F32),
        scratch_shapes=[pltpu.VMEM((2, TOP_K, tm * ROW_SUBLANES, LANES), U32), pltpu.SemaphoreType.DMA((2,))],
        compiler_params=_params(("arbitrary",)),
        name="combine",
    )(dest_flat, dest_flat, xs_mid, gate_tm, mod, final_g, ys_rows, ys2d)


def _rope_tables(seq):
    def cos_sin(pos, dim):
        inv = ROPE_THETA ** (-jnp.arange(0, dim, 2, dtype=F32) / dim)
        ang = pos.astype(F32)[:, None] * inv[None, :]
        ang = jnp.concatenate([ang, ang], axis=-1)
        sign = jnp.where(jnp.arange(dim) < dim // 2, -1.0, 1.0).astype(F32)
        return jnp.cos(ang), jnp.sin(ang) * sign

    t = jnp.arange(seq, dtype=I32)
    cos1, sin1 = cos_sin(t, HEAD_DIM)
    cos_r, sin_r = cos_sin(t // GRID_W, HEAD_DIM // 2)
    cos_c, sin_c = cos_sin(t % GRID_W, HEAD_DIM // 2)
    cosb = jnp.concatenate([cos_r, cos_c], axis=-1)
    sinb = jnp.concatenate([sin_r, sin_c], axis=-1)
    rep = LANES // HEAD_DIM
    return tuple(jnp.tile(a, (1, rep)) for a in (cos1, sin1, cosb, sinb))


def kernel(x, c, w_ada, b_ada, norm1_g, w_in, b_gate, qn_g, kn_g, w_o_dil, w_o_gqa, w_out, norm2_g,
           router_w, router_bias, w_exp_gate, w_exp_up, w_exp_down, w_sh_gate, w_sh_up, w_sh_down, final_g):
    B, S, D = x.shape
    T = B * S
    assert D == D_MODEL and S % (TM_PROJ * 1) == 0 and S // 16 >= QB_DIL
    row = lambda v: v.reshape(1, -1)

    mod = _adaln(c, w_ada, b_ada).reshape(B, 6, D)

    wb = w_in.astype(BF16)
    offs = [0, DIL_WIDTH, 2 * DIL_WIDTH, 3 * DIL_WIDTH, 3 * DIL_WIDTH + GQA_Q_WIDTH,
            3 * DIL_WIDTH + GQA_Q_WIDTH + GQA_KV_WIDTH, 3 * DIL_WIDTH + GQA_Q_WIDTH + 2 * GQA_KV_WIDTH]
    w_qa, w_ka, w_va = (wb[:, offs[i]:offs[i + 1]] for i in range(3))
    w_qb = wb[:, offs[3]:offs[4]].reshape(D, GQA_KV_HEADS, GQA_REP, HEAD_DIM).transpose(0, 2, 1, 3).reshape(D, GQA_Q_WIDTH)
    w_kb, w_vb = wb[:, offs[4]:offs[5]], wb[:, offs[5]:offs[6]]
    w_ga, w_gb = wb[:, offs[6]:offs[6] + D], wb[:, offs[6] + D:offs[6] + 2 * D]
    w_og = w_o_gqa.reshape(GQA_KV_HEADS, GQA_REP, HEAD_DIM, D).transpose(1, 0, 2, 3).reshape(GQA_Q_WIDTH, D).astype(BF16)

    lane_head = lax.broadcasted_iota(I32, (256, 256), 0) // HEAD_DIM
    bd = (lane_head == lane_head.T).astype(BF16)
    qn = jnp.tile(row(qn_g), (1, GQA_Q_WIDTH // HEAD_DIM))
    kn = jnp.tile(row(kn_g), (1, GQA_KV_WIDTH // HEAD_DIM))

    qa, ka, va, qb, kbt, vb, ga, gb = _in_proj(
        x, mod, row(norm1_g), (w_qa, w_ka, w_va, w_qb, w_kb, w_vb, w_ga, w_gb), row(b_gate), qn, kn,
        _rope_tables(S), bd)

    nb_max = -(-(T * TOP_K) // BM_EXPERT) + N_EXPERTS
    nb_max = -(-nb_max // LANES) * LANES
    rows_padded = nb_max * BM_EXPERT

    dil_outs = [_dilated(qa, ka, va, g) for g in range(N_DIL_GROUPS)]
    yb, xs0 = _gqa(qb, kbt, vb, rows_padded)

    xs_mid, h2p, logits = _mix(
        x, dil_outs, yb, ga, gb, mod,
        (w_o_dil.astype(BF16), w_og, w_out.astype(BF16), row(norm2_g), router_w.astype(BF16),
         w_sh_gate.astype(BF16), w_sh_up.astype(BF16), w_sh_down.astype(BF16)))

    idx, gate, gate_tm, rank, counts = _route(logits.reshape(T, N_EXPERTS), router_bias)
    del gate
    dest_tm, bexp, nblk = _plan(counts, idx, rank, nb_max)
    dest_flat = dest_tm[:, :TOP_K].reshape(T * TOP_K)

    xs = _dispatch(dest_flat, h2p.reshape(T, ROW_SUBLANES, LANES), xs0.reshape(rows_padded, ROW_SUBLANES, LANES))
    ys = _experts(bexp.reshape(nb_max), nblk.reshape(LANES)[:1], xs.reshape(rows_padded * ROW_SUBLANES, LANES),
                  w_exp_gate, w_exp_up, w_exp_down, nb_max)
    out = _combine(dest_flat, xs_mid.reshape(T, D), gate_tm, mod, row(final_g), ys, S)
    return out.reshape(B, S, D)
```

```python
import functools

import jax
import jax.numpy as jnp
from jax import lax
from jax.experimental import pallas as pl
from jax.experimental.pallas import tpu as pltpu

F32 = jnp.float32
BF16 = jnp.bfloat16
U32 = jnp.uint32
I32 = jnp.int32

D_MODEL = 1024
HEAD_DIM = 64
DIL_CONFIGS = ((128, 1), (512, 4), (2048, 16))
N_DIL_GROUPS = 3
DIL_GROUP_WIDTH = 256
DIL_WIDTH = N_DIL_GROUPS * DIL_GROUP_WIDTH
DIL_PACKED = DIL_GROUP_WIDTH // 2
DIL_HALF_WINDOW = 64
GQA_Q_WIDTH = 1024
GQA_KV_WIDTH = 256
GQA_KV_HEADS = 4
GQA_REP = 4
GRID_W = 64
ROPE_THETA = 10000.0
N_EXPERTS = 256
TOP_K = 8
N_EXPERT_GROUPS = 8
GROUP_SIZE = N_EXPERTS // N_EXPERT_GROUPS
TOPK_GROUPS = 4
EXPERT_FF = 256
ROUTED_SCALE = 2.5
NORM_EPS = 1e-6
HALF = D_MODEL // 2

LANES = 128
VMEM_LIMIT = 56 * 1024 * 1024

TM_PROJ = 256
TQ_GQA = 256
GQA_SUBTILES = 2
GQA_ITEM_REPS = 2
QB_DIL = 128
TR_ROUTE = 512
TM_DISPATCH = 256
TM_COMBINE = 128
BM_EXPERT = 512

NEG_INF = float("-inf")


def _params(sem):
    return pltpu.CompilerParams(dimension_semantics=sem, vmem_limit_bytes=VMEM_LIMIT)


def _const_spec(shape, single_buffer=False):
    n = len(shape)
    if single_buffer:
        return pl.BlockSpec(shape, lambda *_: (0,) * n, pipeline_mode=pl.Buffered(1))
    return pl.BlockSpec(shape, lambda *_: (0,) * n)


def _adaln_kernel(c_ref, w_ref, b_ref, o_ref):
    c = c_ref[...]
    a = c * jax.nn.sigmoid(c)
    o_ref[...] = jnp.dot(a, w_ref[...], preferred_element_type=F32,
                         precision=lax.Precision.HIGHEST) + b_ref[...]


def _adaln(c, w_ada, b_ada):
    B = c.shape[0]
    return pl.pallas_call(
        _adaln_kernel,
        grid=(6,),
        in_specs=[pl.BlockSpec((B, D_MODEL), lambda j: (0, 0)),
                  pl.BlockSpec((D_MODEL, D_MODEL), lambda j: (0, j)),
                  pl.BlockSpec((1, D_MODEL), lambda j: (0, j))],
        out_specs=pl.BlockSpec((B, D_MODEL), lambda j: (0, j)),
        out_shape=jax.ShapeDtypeStruct((B, 6 * D_MODEL), F32),
        compiler_params=_params(("arbitrary",)),
        name="adaln",
    )(c, w_ada, b_ada.reshape(1, 6 * D_MODEL))


def _tile_lanes(t, width):
    return jnp.concatenate([t] * (width // LANES), axis=1)


def _rope(v, cos, sin_signed, half):
    w = v.shape[-1]
    lane = lax.broadcasted_iota(I32, (1, w), 1)
    first = (lane % (2 * half)) < half
    up = pltpu.roll(v, w - half, axis=1)
    dn = pltpu.roll(v, half, axis=1)
    return v * _tile_lanes(cos, w) + jnp.where(first, up, dn) * _tile_lanes(sin_signed, w)


def _head_rms(v, gain, bd_ref):
    w = v.shape[-1]
    parts = []
    for c0 in range(0, w, 256):
        vc = v[:, c0:c0 + 256]
        ms = jnp.dot((vc * vc).astype(BF16), bd_ref[...], preferred_element_type=F32) * (1.0 / HEAD_DIM)
        parts.append(vc * lax.rsqrt(ms + NORM_EPS))
    return jnp.concatenate(parts, axis=1) * gain


def _in_proj_kernel(x_ref, mod_ref, g1_ref, wqa_ref, wka_ref, wva_ref, wqb_ref, wkb_ref, wvb_ref,
                    wga_ref, wgb_ref, bg_ref, qn_ref, kn_ref, cos1_ref, sin1_ref, cosb_ref, sinb_ref,
                    bd_ref, qa_ref, ka_ref, va_ref, qb_ref, kbt_ref, vb_ref, ga_ref, gb_ref, zero_ref):
    zero_ref[...] = jnp.zeros_like(zero_ref)
    x = x_ref[0]
    ms = jnp.mean(x * x, axis=-1, keepdims=True)
    h = x * lax.rsqrt(ms + NORM_EPS) * g1_ref[...]
    h = h * (1.0 + mod_ref[0, 1:2, :]) + mod_ref[0, 0:1, :]
    hb = h.astype(BF16)

    def proj(w_ref):
        return jnp.dot(hb, w_ref[...], preferred_element_type=F32)

    cos1, sin1 = cos1_ref[...], sin1_ref[...]
    cosb, sinb = cosb_ref[...], sinb_ref[...]

    qa = _rope(proj(wqa_ref), cos1, sin1, HEAD_DIM // 2) * (HEAD_DIM ** -0.5)
    ka = _rope(proj(wka_ref), cos1, sin1, HEAD_DIM // 2)
    va = proj(wva_ref)
    for g in range(N_DIL_GROUPS):
        sl = slice(g * DIL_GROUP_WIDTH, (g + 1) * DIL_GROUP_WIDTH)
        qa_ref[g, 0] = _pack_halves(qa[:, sl])
        ka_ref[g, 0] = _pack_halves(ka[:, sl])
        va_ref[g, 0] = _pack_halves(va[:, sl])

    qb = _head_rms(proj(wqb_ref), qn_ref[...], bd_ref)
    qb_ref[0] = (_rope(qb, cosb, sinb, HEAD_DIM // 4) * (HEAD_DIM ** -0.5)).astype(BF16)
    kb = _rope(_head_rms(proj(wkb_ref), kn_ref[...], bd_ref), cosb, sinb, HEAD_DIM // 4)
    kbt_ref[0] = kb.T.astype(BF16)
    vb_ref[0] = proj(wvb_ref).astype(BF16)

    ga_ref[0] = jax.nn.sigmoid(proj(wga_ref) + bg_ref[:, :D_MODEL]).astype(BF16)
    gb_ref[0] = jax.nn.sigmoid(proj(wgb_ref) + bg_ref[:, D_MODEL:]).astype(BF16)


def _in_proj(x, mod, g1, weights, bgate, qn, kn, tables, bd, rows_padded):
    B, S, _ = x.shape
    tm = TM_PROJ
    steps = B * (S // tm)
    assert rows_padded % steps == 0
    zrows = rows_padded // steps
    tok = lambda w: pl.BlockSpec((1, tm, w), lambda b, i: (b, i, 0))
    tab = pl.BlockSpec((tm, LANES), lambda b, i: (i, 0))
    dil_out = pl.BlockSpec((N_DIL_GROUPS, 1, tm, DIL_PACKED), lambda b, i: (0, b, i, 0))
    dil_shape = jax.ShapeDtypeStruct((N_DIL_GROUPS, B, S, DIL_PACKED), U32)
    in_specs = ([tok(D_MODEL), pl.BlockSpec((1, 6, D_MODEL), lambda b, i: (b, 0, 0)), _const_spec((1, D_MODEL))]
                + [_const_spec(w.shape, single_buffer=True) for w in weights]
                + [_const_spec(bgate.shape), _const_spec(qn.shape), _const_spec(kn.shape), tab, tab, tab, tab,
                   _const_spec(bd.shape)])
    out_specs = [dil_out, dil_out, dil_out, tok(GQA_Q_WIDTH),
                 pl.BlockSpec((1, GQA_KV_WIDTH, tm), lambda b, i: (b, 0, i)), tok(GQA_KV_WIDTH),
                 tok(D_MODEL), tok(D_MODEL),
                 pl.BlockSpec((zrows * ROW_SUBLANES, LANES), lambda b, i: (b * (S // tm) + i, 0))]
    out_shape = [dil_shape, dil_shape, dil_shape,
                 jax.ShapeDtypeStruct((B, S, GQA_Q_WIDTH), BF16),
                 jax.ShapeDtypeStruct((B, GQA_KV_WIDTH, S), BF16),
                 jax.ShapeDtypeStruct((B, S, GQA_KV_WIDTH), BF16),
                 jax.ShapeDtypeStruct((B, S, D_MODEL), BF16),
                 jax.ShapeDtypeStruct((B, S, D_MODEL), BF16),
                 jax.ShapeDtypeStruct((rows_padded * ROW_SUBLANES, LANES), U32)]
    return pl.pallas_call(
        _in_proj_kernel,
        grid=(B, S // tm),
        in_specs=in_specs,
        out_specs=out_specs,
        out_shape=out_shape,
        compiler_params=_params(("arbitrary", "arbitrary")),
        name="in_proj",
    )(x, mod, g1, *weights, bgate, qn, kn, *tables, bd)


def _dil_kernel(q_ref, k_ref, v_ref, o_ref, lse_lo_ref, lse_hi_ref, *, dil, seq):
    n = DIL_HALF_WINDOW
    qb = min(QB_DIL, seq)
    kw = min(qb + 2 * n, seq)
    n_heads = DIL_GROUP_WIDTH // HEAD_DIM
    lane = lax.broadcasted_iota(I32, (1, DIL_GROUP_WIDTH), 1)
    head_masks = [(lane >= h * HEAD_DIM) & (lane < (h + 1) * HEAD_DIM) for h in range(n_heads)]

    def rows(ref, start, count, r):
        if dil == 1:
            return ref[pl.ds(start, count), :]
        return ref[pl.ds(start * dil + r, count, stride=dil), :]

    def load(ref, start, count, r):
        lo, hi = _unpack_halves(rows(ref, start, count, r))
        return jnp.concatenate([lo, hi], axis=1).astype(BF16)

    def block(i, r):
        a = pl.multiple_of(i * qb, qb)
        ws = pl.multiple_of(jnp.clip(a - n, 0, seq - kw), n)
        q, k, v = load(q_ref, a, qb, r), load(k_ref, ws, kw, r), load(v_ref, ws, kw, r)
        qs = jnp.concatenate([jnp.where(hm, q, jnp.zeros_like(q)) for hm in head_masks], axis=0)
        s = lax.dot_general(qs, k, (((1,), (1,)), ((), ())), preferred_element_type=F32)
        qpos = a + lax.broadcasted_iota(I32, (qb, 1), 0)
        kpos = ws + lax.broadcasted_iota(I32, (1, kw), 1)
        valid = jnp.abs(kpos - qpos) <= n
        s = jnp.where(jnp.concatenate([valid] * n_heads, axis=0), s, NEG_INF)
        m = jnp.max(s, axis=1, keepdims=True)
        p = jnp.exp(s - m)
        l = jnp.sum(p, axis=1, keepdims=True)
        pv = jnp.dot(p.astype(BF16), v, preferred_element_type=F32) * (1.0 / l)
        lse = m + jnp.log(l)
        o_acc = jnp.zeros((qb, DIL_GROUP_WIDTH), F32)
        lse_acc = jnp.zeros((qb, DIL_GROUP_WIDTH), F32)
        for h, hm in enumerate(head_masks):
            o_acc = jnp.where(hm, pv[h * qb:(h + 1) * qb], o_acc)
            lse_acc = jnp.where(hm, lse[h * qb:(h + 1) * qb], lse_acc)
        packed = _pack_halves(o_acc)
        if dil == 1:
            dst = pl.ds(a, qb)
        else:
            dst = pl.ds(a * dil + r, qb, stride=dil)
        o_ref[dst, :] = packed
        lse_lo_ref[dst, :] = lse_acc[:, :DIL_PACKED]
        lse_hi_ref[dst, :] = lse_acc[:, DIL_PACKED:]

    for r in range(dil):
        def body(i, carry, r=r):
            block(i, r)
            return carry
        lax.fori_loop(0, seq // qb, body, 0)


def _dilated(qa, ka, va, g):
    _, B, S, W = qa.shape
    dil = DIL_CONFIGS[g][1]
    in_spec = pl.BlockSpec((None, None, S, W), lambda b: (g, b, 0, 0))
    out_spec = pl.BlockSpec((None, S, W), lambda b: (b, 0, 0))
    return pl.pallas_call(
        functools.partial(_dil_kernel, dil=dil, seq=S // dil),
        grid=(B,),
        in_specs=[in_spec, in_spec, in_spec],
        out_specs=[out_spec, out_spec, out_spec],
        out_shape=[jax.ShapeDtypeStruct((B, S, W), U32), jax.ShapeDtypeStruct((B, S, W), F32),
                   jax.ShapeDtypeStruct((B, S, W), F32)],
        compiler_params=_params(("arbitrary",)),
        name=f"dilated{g}",
    )(qa, ka, va)


def _gqa_kernel(q_ref, kt_ref, v_ref, o_ref, s0_ref, s1_ref, p0_ref, p1_ref):
    tq = TQ_GQA
    s_refs, p_refs = (s0_ref, s1_ref), (p0_ref, p1_ref)
    lane = lax.broadcasted_iota(I32, (1, GQA_KV_WIDTH), 1)
    head_masks = [(lane >= g * HEAD_DIM) & (lane < (g + 1) * HEAD_DIM) for g in range(GQA_KV_HEADS)]
    nrep = GQA_ITEM_REPS
    items = [(sub, c0, g) for sub in range(q_ref.shape[0] // tq) for c0 in range(0, GQA_REP, nrep)
             for g in range(GQA_KV_HEADS)]

    def scores(i):
        sub, c0, g = items[i]
        rows = slice(sub * tq, (sub + 1) * tq)
        qg = jnp.concatenate(
            [jnp.where(head_masks[g], q_ref[rows, c * 256:(c + 1) * 256], jnp.zeros((tq, 256), BF16))
             for c in range(c0, c0 + nrep)], axis=0)
        s_refs[i % 2][...] = jnp.dot(qg, kt_ref[...], preferred_element_type=F32)

    scores(0)
    outs = None
    for i, (sub, c0, g) in enumerate(items):
        if i + 1 < len(items):
            scores(i + 1)
        if g == 0:
            outs = [jnp.zeros((tq, GQA_KV_WIDTH), F32) for _ in range(nrep)]
        s = s_refs[i % 2][...]
        m = jnp.max(s, axis=1, keepdims=True)
        e = jnp.exp(s - m)
        l = jnp.sum(e, axis=1, keepdims=True)
        p_refs[i % 2][...] = e.astype(BF16)
        pv = jnp.dot(p_refs[i % 2][...], v_ref[...], preferred_element_type=F32) * (1.0 / l)
        for j in range(nrep):
            outs[j] = jnp.where(head_masks[g], pv[j * tq:(j + 1) * tq], outs[j])
        if g == GQA_KV_HEADS - 1:
            rows = slice(sub * tq, (sub + 1) * tq)
            for j in range(nrep):
                o_ref[rows, (c0 + j) * 256:(c0 + j + 1) * 256] = outs[j].astype(BF16)


def _gqa(qb, kbt, vb):
    B, S, _ = qb.shape
    tq = TQ_GQA * GQA_SUBTILES
    item_rows = TQ_GQA * GQA_ITEM_REPS
    return pl.pallas_call(
        _gqa_kernel,
        grid=(B, S // tq),
        in_specs=[pl.BlockSpec((None, tq, GQA_Q_WIDTH), lambda b, i: (b, i, 0)),
                  pl.BlockSpec((None, GQA_KV_WIDTH, S), lambda b, i: (b, 0, 0)),
                  pl.BlockSpec((None, S, GQA_KV_WIDTH), lambda b, i: (b, 0, 0))],
        out_specs=pl.BlockSpec((None, tq, GQA_Q_WIDTH), lambda b, i: (b, i, 0)),
        out_shape=jax.ShapeDtypeStruct((B, S, GQA_Q_WIDTH), BF16),
        scratch_shapes=[pltpu.VMEM((item_rows, S), F32), pltpu.VMEM((item_rows, S), F32),
                        pltpu.VMEM((item_rows, S), BF16), pltpu.VMEM((item_rows, S), BF16)],
        compiler_params=_params(("arbitrary", "arbitrary")),
        name="gqa",
    )(qb, kbt, vb)


def _pack_halves(y):
    half = y.shape[1] // 2
    lo = lax.bitcast_convert_type(y[:, :half].astype(BF16).astype(F32), U32)
    hi = lax.bitcast_convert_type(y[:, half:].astype(BF16).astype(F32), U32)
    return (lo >> 16) | hi


def _unpack_halves(u):
    lo = lax.bitcast_convert_type(u << 16, F32)
    hi = lax.bitcast_convert_type(u & jnp.uint32(0xFFFF0000), F32)
    return lo, hi


ROW_SUBLANES = HALF // LANES


def _store_rows(ref, packed):
    m = packed.shape[0]
    for c in range(ROW_SUBLANES):
        ref[pl.ds(c, m, stride=ROW_SUBLANES), :] = packed[:, c * LANES:(c + 1) * LANES]


def _load_rows(ref, m):
    return jnp.concatenate([ref[pl.ds(c, m, stride=ROW_SUBLANES), :] for c in range(ROW_SUBLANES)], axis=1)


def _mix_kernel(x_ref, *refs):
    dil_refs, refs = refs[:3 * N_DIL_GROUPS], refs[3 * N_DIL_GROUPS:]
    (yb_ref, ga_ref, gb_ref, mod_ref, wod_ref, wog_ref, wout_ref, g2_ref, rw_ref, wsg_ref, wsu_ref, wsd_ref,
     xs_ref, h2_ref, logit_ref) = refs

    outs = [_unpack_halves(dil_refs[3 * g][0]) for g in range(N_DIL_GROUPS)]
    halves = []
    for part in range(2):
        lses = [dil_refs[3 * g + 1 + part][0] for g in range(N_DIL_GROUPS)]
        mx = functools.reduce(jnp.maximum, lses)
        es = [jnp.exp(l - mx) for l in lses]
        num = sum(e * outs[g][part] for g, e in enumerate(es))
        halves.append(num / sum(es))
    ya = jnp.concatenate(halves, axis=1)
    y_a = jnp.dot(ya.astype(BF16), wod_ref[...], preferred_element_type=F32)
    y_b = jnp.dot(yb_ref[0], wog_ref[...], preferred_element_type=F32)
    mix = ga_ref[0].astype(F32) * y_a + gb_ref[0].astype(F32) * y_b
    mixed = jnp.dot(mix.astype(BF16), wout_ref[...], preferred_element_type=F32)
    x1 = x_ref[0] + mod_ref[0, 2:3, :] * mixed

    ms = jnp.mean(x1 * x1, axis=-1, keepdims=True)
    h2 = x1 * lax.rsqrt(ms + NORM_EPS) * g2_ref[...]
    h2 = h2 * (1.0 + mod_ref[0, 4:5, :]) + mod_ref[0, 3:4, :]
    h2b = h2.astype(BF16)
    _store_rows(h2_ref, _pack_halves(h2))
    logit_ref[0] = jnp.dot(h2b, rw_ref[...], preferred_element_type=F32)

    gt = jnp.dot(h2b, wsg_ref[...], preferred_element_type=F32)
    up = jnp.dot(h2b, wsu_ref[...], preferred_element_type=F32)
    act = (gt * jax.nn.sigmoid(gt) * up).astype(BF16)
    shared = jnp.dot(act, wsd_ref[...], preferred_element_type=F32)
    xs_ref[0] = x1 + mod_ref[0, 5:6, :] * shared


def _mix(x, dil_outs, yb, ga, gb, mod, weights):
    B, S, _ = x.shape
    tm = TM_PROJ
    tok = lambda w: pl.BlockSpec((1, tm, w), lambda b, i: (b, i, 0))
    dil_flat = [a for group in dil_outs for a in group]
    in_specs = ([tok(D_MODEL)] + [tok(DIL_PACKED)] * len(dil_flat) + [tok(GQA_Q_WIDTH), tok(D_MODEL), tok(D_MODEL),
                pl.BlockSpec((1, 6, D_MODEL), lambda b, i: (b, 0, 0))] + [_const_spec(w.shape) for w in weights])
    return pl.pallas_call(
        _mix_kernel,
        grid=(B, S // tm),
        in_specs=in_specs,
        out_specs=[tok(D_MODEL),
                   pl.BlockSpec((tm * ROW_SUBLANES, LANES), lambda b, i: (b * (S // tm) + i, 0)),
                   tok(N_EXPERTS)],
        out_shape=[jax.ShapeDtypeStruct((B, S, D_MODEL), F32),
                   jax.ShapeDtypeStruct((B * S * ROW_SUBLANES, LANES), U32),
                   jax.ShapeDtypeStruct((B, S, N_EXPERTS), F32)],
        compiler_params=_params(("arbitrary", "arbitrary")),
        name="mix",
    )(x, *dil_flat, yb, ga, gb, mod, *weights)


def _first_argmax(v, idx, big):
    m = jnp.max(v, axis=0, keepdims=True)
    i = jnp.min(jnp.where(v == m, idx, big), axis=0, keepdims=True)
    return m, i


def _route_kernel(logit_ref, bias_ref, tri_ref, idx_ref, gate_ref, gate_tm_ref, rank_ref, cnt_ref, base_ref):
    step = pl.program_id(0)
    tr = logit_ref.shape[0]

    @pl.when(step == 0)
    def _():
        base_ref[...] = jnp.zeros_like(base_ref)

    scores = jax.nn.sigmoid(logit_ref[...].T)
    biased = scores + bias_ref[...]
    row = lax.broadcasted_iota(I32, (N_EXPERTS, tr), 0)

    gscore = []
    group_row = lax.broadcasted_iota(I32, (GROUP_SIZE, tr), 0)
    for g in range(N_EXPERT_GROUPS):
        v, ri = biased[g * GROUP_SIZE:(g + 1) * GROUP_SIZE], group_row + g * GROUP_SIZE
        m1, i1 = _first_argmax(v, ri, N_EXPERTS)
        m2 = jnp.max(jnp.where(ri == i1, NEG_INF, v), axis=0, keepdims=True)
        gscore.append(m1 + m2)
    cur = jnp.concatenate(gscore, axis=0)
    gi = lax.broadcasted_iota(I32, (N_EXPERT_GROUPS, tr), 0)
    sel = jnp.zeros((N_EXPERT_GROUPS, tr), F32)
    for _ in range(TOPK_GROUPS):
        _, i = _first_argmax(cur, gi, N_EXPERT_GROUPS)
        hit = gi == i
        sel = jnp.where(hit, 1.0, sel)
        cur = jnp.where(hit, NEG_INF, cur)

    cur = jnp.concatenate(
        [jnp.where(sel[g:g + 1] > 0, biased[g * GROUP_SIZE:(g + 1) * GROUP_SIZE], NEG_INF)
         for g in range(N_EXPERT_GROUPS)], axis=0)
    idxs, gates, hits = [], [], []
    assigned = jnp.zeros((N_EXPERTS, tr), F32)
    for _ in range(TOP_K):
        _, i = _first_argmax(cur, row, N_EXPERTS)
        hit = row == i
        gates.append(jnp.sum(jnp.where(hit, scores, 0.0), axis=0, keepdims=True))
        cur = jnp.where(hit, NEG_INF, cur)
        assigned = jnp.where(hit, 1.0, assigned)
        idxs.append(i)
        hits.append(hit)
    gate = jnp.concatenate(gates, axis=0)
    gate = gate / jnp.sum(gate, axis=0, keepdims=True) * ROUTED_SCALE

    before = jnp.dot(assigned.astype(BF16), tri_ref[...], preferred_element_type=F32) + base_ref[:, 0:1]
    ranks = [jnp.sum(jnp.where(h, before, 0.0), axis=0, keepdims=True) for h in hits]

    idx_ref[...] = jnp.concatenate(idxs, axis=0)
    gate_ref[...] = gate
    rank_ref[...] = jnp.concatenate(ranks, axis=0).astype(I32)
    gate_tm_ref[...] = jnp.concatenate([gate, jnp.zeros((LANES - TOP_K, tr), F32)], axis=0).T
    base_ref[...] = base_ref[...] + jnp.sum(assigned, axis=1, keepdims=True)
    cnt_ref[...] = base_ref[...]


def _route(logits, bias):
    T = logits.shape[0]
    tr = TR_ROUTE
    tri = (lax.broadcasted_iota(I32, (tr, tr), 0) < lax.broadcasted_iota(I32, (tr, tr), 1)).astype(BF16)
    slot = pl.BlockSpec((TOP_K, tr), lambda i: (0, i))
    return pl.pallas_call(
        _route_kernel,
        grid=(T // tr,),
        in_specs=[pl.BlockSpec((tr, N_EXPERTS), lambda i: (i, 0)), _const_spec((N_EXPERTS, 1)), _const_spec((tr, tr))],
        out_specs=[slot, slot, pl.BlockSpec((tr, LANES), lambda i: (i, 0)), slot, _const_spec((N_EXPERTS, LANES))],
        out_shape=[jax.ShapeDtypeStruct((TOP_K, T), I32), jax.ShapeDtypeStruct((TOP_K, T), F32),
                   jax.ShapeDtypeStruct((T, LANES), F32), jax.ShapeDtypeStruct((TOP_K, T), I32),
                   jax.ShapeDtypeStruct((N_EXPERTS, LANES), F32)],
        scratch_shapes=[pltpu.VMEM((N_EXPERTS, LANES), F32)],
        compiler_params=_params(("arbitrary",)),
        name="route",
    )(logits, bias.reshape(N_EXPERTS, 1), tri)


def _plan_kernel(cnt_ref, idx_ref, rank_ref, ltri_ref, dest_ref, bexp_ref, nblk_ref, *, nb_max):
    tr = idx_ref.shape[1]
    blocks = jnp.floor((cnt_ref[...] + (BM_EXPERT - 1)) * (1.0 / BM_EXPERT))
    start = jnp.dot(ltri_ref[...], blocks.astype(BF16), preferred_element_type=F32)
    end = start + blocks
    blk = lax.broadcasted_iota(I32, (1, nb_max), 1).astype(F32)
    bexp = jnp.sum(jnp.where(end[:, 0:1] <= blk, 1.0, 0.0), axis=0, keepdims=True)
    bexp_ref[...] = jnp.minimum(bexp, N_EXPERTS - 1.0).astype(I32)
    nblk_ref[...] = jnp.sum(blocks, axis=0, keepdims=True).astype(I32)

    row = lax.broadcasted_iota(I32, (N_EXPERTS, tr), 0)
    start_rows = start[:, 0:1] * float(BM_EXPERT)
    dests = []
    for k in range(TOP_K):
        hit = row == idx_ref[k:k + 1, :]
        dests.append(jnp.sum(jnp.where(hit, start_rows, 0.0), axis=0, keepdims=True))
    dest = jnp.concatenate(dests, axis=0) + rank_ref[...].astype(F32)
    dest_ref[...] = jnp.concatenate([dest, jnp.zeros((LANES - TOP_K, tr), F32)], axis=0).T.astype(I32)


def _plan(counts, idx, rank, nb_max):
    T = idx.shape[1]
    tr = TR_ROUTE
    ltri = (lax.broadcasted_iota(I32, (N_EXPERTS, N_EXPERTS), 1)
            < lax.broadcasted_iota(I32, (N_EXPERTS, N_EXPERTS), 0)).astype(BF16)
    slot = pl.BlockSpec((TOP_K, tr), lambda i: (0, i))
    return pl.pallas_call(
        functools.partial(_plan_kernel, nb_max=nb_max),
        grid=(T // tr,),
        in_specs=[_const_spec((N_EXPERTS, LANES)), slot, slot, _const_spec((N_EXPERTS, N_EXPERTS))],
        out_specs=[pl.BlockSpec((tr, LANES), lambda i: (i, 0)), _const_spec((1, nb_max)), _const_spec((1, LANES))],
        out_shape=[jax.ShapeDtypeStruct((T, LANES), I32), jax.ShapeDtypeStruct((1, nb_max), I32),
                   jax.ShapeDtypeStruct((1, LANES), I32)],
        compiler_params=_params(("arbitrary",)),
        name="plan",
    )(counts, idx, rank, ltri)


def _dispatch_kernel(dest_ref, h_ref, xs_in_ref, xs_ref, sem):
    del xs_in_ref
    tm = h_ref.shape[0]

    def issue(t, carry):
        for k in range(TOP_K):
            pltpu.make_async_copy(h_ref.at[t], xs_ref.at[dest_ref[t * TOP_K + k]], sem).start(priority=k % 2)
        return carry

    lax.fori_loop(0, tm, issue, 0)
    for _ in range(TOP_K):
        pltpu.make_async_copy(h_ref, xs_ref.at[pl.ds(0, tm)], sem).wait()


def _dispatch(dest_flat, h_rows, xs0):
    T = h_rows.shape[0]
    rows_padded = xs0.shape[0]
    tm = TM_DISPATCH
    return pl.pallas_call(
        _dispatch_kernel,
        grid=(T // tm,),
        in_specs=[pl.BlockSpec((tm * TOP_K,), lambda i: (i,), memory_space=pltpu.SMEM),
                  pl.BlockSpec((tm, ROW_SUBLANES, LANES), lambda i: (i, 0, 0)),
                  pl.BlockSpec(memory_space=pl.ANY)],
        out_specs=pl.BlockSpec(memory_space=pl.ANY),
        out_shape=jax.ShapeDtypeStruct((rows_padded, ROW_SUBLANES, LANES), U32),
        scratch_shapes=[pltpu.SemaphoreType.DMA(())],
        input_output_aliases={2: 0},
        compiler_params=_params(("arbitrary",)),
        name="dispatch",
    )(dest_flat, h_rows, xs0)


def _expert_kernel(bexp_ref, nblk_ref, xs_ref, wg_ref, wu_ref, wd_ref, ys_ref, wgb_ref, wub_ref, wdb_ref):
    p = pl.program_id(0)
    prev = bexp_ref[jnp.maximum(p - 1, 0)]
    active = p < nblk_ref[0]

    @pl.when(active & ((p == 0) | (bexp_ref[p] != prev)))
    def _():
        wgb_ref[...] = wg_ref[...].astype(BF16)
        wub_ref[...] = wu_ref[...].astype(BF16)
        wdb_ref[...] = wd_ref[...].astype(BF16)

    @pl.when(active)
    def _():
        lo, hi = _unpack_halves(_load_rows(xs_ref, BM_EXPERT))
        lo, hi = lo.astype(BF16), hi.astype(BF16)

        def proj(w_ref):
            return (jnp.dot(lo, w_ref[:HALF, :], preferred_element_type=F32)
                    + jnp.dot(hi, w_ref[HALF:, :], preferred_element_type=F32))

        gt, up = proj(wgb_ref), proj(wub_ref)
        act = (gt * jax.nn.sigmoid(gt) * up).astype(BF16)
        _store_rows(ys_ref, _pack_halves(jnp.dot(act, wdb_ref[...], preferred_element_type=F32)))


def _experts(bexp, nblk, xs, w_gate, w_up, w_down, nb_max):
    bm = BM_EXPERT
    rows = lambda p, be, nb: (jnp.minimum(p, nb[0] - 1), 0)
    wspec = lambda shape: pl.BlockSpec((None,) + shape, lambda p, be, nb: (be[p], 0, 0))
    grid_spec = pltpu.PrefetchScalarGridSpec(
        num_scalar_prefetch=2,
        grid=(nb_max,),
        in_specs=[pl.BlockSpec((bm * ROW_SUBLANES, LANES), rows), wspec((D_MODEL, EXPERT_FF)),
                  wspec((D_MODEL, EXPERT_FF)), wspec((EXPERT_FF, D_MODEL))],
        out_specs=pl.BlockSpec((bm * ROW_SUBLANES, LANES), rows),
        scratch_shapes=[pltpu.VMEM((D_MODEL, EXPERT_FF), BF16), pltpu.VMEM((D_MODEL, EXPERT_FF), BF16),
                        pltpu.VMEM((EXPERT_FF, D_MODEL), BF16)],
    )
    return pl.pallas_call(
        _expert_kernel,
        grid_spec=grid_spec,
        out_shape=jax.ShapeDtypeStruct(xs.shape, U32),
        input_output_aliases={2: 0},
        compiler_params=_params(("arbitrary",)),
        name="experts",
    )(bexp, nblk, xs, w_gate, w_up, w_down)


def _combine_kernel(dest_ref, dnext_ref, xs_ref, gate_ref, mod_ref, fg_ref, ys_ref, ys2d_ref, o_ref, buf_ref, sem):
    i = pl.program_id(0)
    n = pl.num_programs(0)
    tm = o_ref.shape[0]
    slot = i % 2

    def row_copy(d_ref, s, t, k):
        return pltpu.make_async_copy(ys_ref.at[d_ref[t * TOP_K + k]],
                                     buf_ref.at[s, k, pl.ds(t * ROW_SUBLANES, ROW_SUBLANES), :], sem.at[s])

    @pl.when(i == 0)
    def _():
        def body(t, carry):
            for k in range(TOP_K):
                row_copy(dest_ref, 0, t, k).start(priority=k % 2)
            return carry
        lax.fori_loop(0, tm, body, 0)

    def finish(prefetch):
        for k in range(TOP_K):
            pltpu.make_async_copy(ys2d_ref.at[pl.ds(0, tm * ROW_SUBLANES), :], buf_ref.at[slot, k],
                                  sem.at[slot]).wait()
        acc_lo = jnp.zeros((tm, HALF), F32)
        acc_hi = jnp.zeros((tm, HALF), F32)
        batch = tm // TOP_K
        for k in range(TOP_K):
            if prefetch:
                for t in range(k * batch, (k + 1) * batch):
                    for kk in range(TOP_K):
                        row_copy(dnext_ref, 1 - slot, t, kk).start(priority=kk % 2)
            lo, hi = _unpack_halves(_load_rows(buf_ref.at[slot, k], tm))
            g = gate_ref[:, k:k + 1]
            acc_lo = acc_lo + g * lo
            acc_hi = acc_hi + g * hi
        routed = jnp.concatenate([acc_lo, acc_hi], axis=1)
        x2 = xs_ref[...] + mod_ref[0, 5:6, :] * routed
        ms = jnp.mean(x2 * x2, axis=-1, keepdims=True)
        o_ref[...] = x2 * lax.rsqrt(ms + NORM_EPS) * fg_ref[...]

    @pl.when(i + 1 < n)
    def _():
        finish(True)

    @pl.when(i + 1 >= n)
    def _():
        finish(False)


def _combine(dest_flat, xs_mid, gate_tm, mod, final_g, ys2d, seq):
    T = xs_mid.shape[0]
    tm = TM_COMBINE
    n = T // tm
    per_seq = seq // tm
    ys_rows = ys2d.reshape(-1, ROW_SUBLANES, LANES)
    return pl.pallas_call(
        _combine_kernel,
        grid=(n,),
        in_specs=[pl.BlockSpec((tm * TOP_K,), lambda i: (i,), memory_space=pltpu.SMEM),
                  pl.BlockSpec((tm * TOP_K,), lambda i: (jnp.minimum(i + 1, n - 1),), memory_space=pltpu.SMEM),
                  pl.BlockSpec((tm, D_MODEL), lambda i: (i, 0)),
                  pl.BlockSpec((tm, LANES), lambda i: (i, 0)),
                  pl.BlockSpec((1, 6, D_MODEL), lambda i: (i // per_seq, 0, 0)),
                  _const_spec((1, D_MODEL)),
                  pl.BlockSpec(memory_space=pl.ANY),
                  pl.BlockSpec(memory_space=pl.ANY)],
        out_specs=pl.BlockSpec((tm, D_MODEL), lambda i: (i, 0)),
        out_shape=jax.ShapeDtypeStruct((T, D_MODEL), F32),
        scratch_shapes=[pltpu.VMEM((2, TOP_K, tm * ROW_SUBLANES, LANES), U32), pltpu.SemaphoreType.DMA((2,))],
        compiler_params=_params(("arbitrary",)),
        name="combine",
    )(dest_flat, dest_flat, xs_mid, gate_tm, mod, final_g, ys_rows, ys2d)


def _rope_tables(seq):
    def cos_sin(pos, dim):
        inv = ROPE_THETA ** (-jnp.arange(0, dim, 2, dtype=F32) / dim)
        ang = pos.astype(F32)[:, None] * inv[None, :]
        ang = jnp.concatenate([ang, ang], axis=-1)
        sign = jnp.where(jnp.arange(dim) < dim // 2, -1.0, 1.0).astype(F32)
        return jnp.cos(ang), jnp.sin(ang) * sign

    t = jnp.arange(seq, dtype=I32)
    cos1, sin1 = cos_sin(t, HEAD_DIM)
    cos_r, sin_r = cos_sin(t // GRID_W, HEAD_DIM // 2)
    cos_c, sin_c = cos_sin(t % GRID_W, HEAD_DIM // 2)
    cosb = jnp.concatenate([cos_r, cos_c], axis=-1)
    sinb = jnp.concatenate([sin_r, sin_c], axis=-1)
    rep = LANES // HEAD_DIM
    return tuple(jnp.tile(a, (1, rep)) for a in (cos1, sin1, cosb, sinb))


def kernel(x, c, w_ada, b_ada, norm1_g, w_in, b_gate, qn_g, kn_g, w_o_dil, w_o_gqa, w_out, norm2_g,
           router_w, router_bias, w_exp_gate, w_exp_up, w_exp_down, w_sh_gate, w_sh_up, w_sh_down, final_g):
    B, S, D = x.shape
    T = B * S
    assert D == D_MODEL and S % (TM_PROJ * 1) == 0 and S // 16 >= QB_DIL
    row = lambda v: v.reshape(1, -1)

    mod = _adaln(c, w_ada, b_ada).reshape(B, 6, D)

    wb = w_in.astype(BF16)
    offs = [0, DIL_WIDTH, 2 * DIL_WIDTH, 3 * DIL_WIDTH, 3 * DIL_WIDTH + GQA_Q_WIDTH,
            3 * DIL_WIDTH + GQA_Q_WIDTH + GQA_KV_WIDTH, 3 * DIL_WIDTH + GQA_Q_WIDTH + 2 * GQA_KV_WIDTH]
    w_qa, w_ka, w_va = (wb[:, offs[i]:offs[i + 1]] for i in range(3))
    w_qb = wb[:, offs[3]:offs[4]].reshape(D, GQA_KV_HEADS, GQA_REP, HEAD_DIM).transpose(0, 2, 1, 3).reshape(D, GQA_Q_WIDTH)
    w_kb, w_vb = wb[:, offs[4]:offs[5]], wb[:, offs[5]:offs[6]]
    w_ga, w_gb = wb[:, offs[6]:offs[6] + D], wb[:, offs[6] + D:offs[6] + 2 * D]
    w_og = w_o_gqa.reshape(GQA_KV_HEADS, GQA_REP, HEAD_DIM, D).transpose(1, 0, 2, 3).reshape(GQA_Q_WIDTH, D).astype(BF16)

    lane_head = lax.broadcasted_iota(I32, (256, 256), 0) // HEAD_DIM
    bd = (lane_head == lane_head.T).astype(BF16)
    qn = jnp.tile(row(qn_g), (1, GQA_Q_WIDTH // HEAD_DIM))
    kn = jnp.tile(row(kn_g), (1, GQA_KV_WIDTH // HEAD_DIM))

    nb_max = -(-(T * TOP_K) // BM_EXPERT) + N_EXPERTS
    nb_max = -(-nb_max // LANES) * LANES
    rows_padded = nb_max * BM_EXPERT

    qa, ka, va, qb, kbt, vb, ga, gb, xs0 = _in_proj(
        x, mod, row(norm1_g), (w_qa, w_ka, w_va, w_qb, w_kb, w_vb, w_ga, w_gb), row(b_gate), qn, kn,
        _rope_tables(S), bd, rows_padded)

    dil_outs = [_dilated(qa, ka, va, g) for g in range(N_DIL_GROUPS)]
    yb = _gqa(qb, kbt, vb)

    xs_mid, h2p, logits = _mix(
        x, dil_outs, yb, ga, gb, mod,
        (w_o_dil.astype(BF16), w_og, w_out.astype(BF16), row(norm2_g), router_w.astype(BF16),
         w_sh_gate.astype(BF16), w_sh_up.astype(BF16), w_sh_down.astype(BF16)))

    idx, gate, gate_tm, rank, counts = _route(logits.reshape(T, N_EXPERTS), router_bias)
    del gate
    dest_tm, bexp, nblk = _plan(counts, idx, rank, nb_max)
    dest_flat = dest_tm[:, :TOP_K].reshape(T * TOP_K)

    xs = _dispatch(dest_flat, h2p.reshape(T, ROW_SUBLANES, LANES), xs0.reshape(rows_padded, ROW_SUBLANES, LANES))
    ys = _experts(bexp.reshape(nb_max), nblk.reshape(LANES)[:1], xs.reshape(rows_padded * ROW_SUBLANES, LANES),
                  w_exp_gate, w_exp_up, w_exp_down, nb_max)
    out = _combine(dest_flat, xs_mid.reshape(T, D), gate_tm, mod, row(final_g), ys, S)
    return out.reshape(B, S, D)
```

```python
import functools

import jax
import jax.numpy as jnp
from jax import lax
from jax.experimental import pallas as pl
from jax.experimental.pallas import tpu as pltpu

F32 = jnp.float32
BF16 = jnp.bfloat16
U32 = jnp.uint32
I32 = jnp.int32

D_MODEL = 1024
HEAD_DIM = 64
DIL_CONFIGS = ((128, 1), (512, 4), (2048, 16))
N_DIL_GROUPS = 3
DIL_GROUP_WIDTH = 256
DIL_WIDTH = N_DIL_GROUPS * DIL_GROUP_WIDTH
DIL_PACKED = DIL_GROUP_WIDTH // 2
DIL_HALF_WINDOW = 64
GQA_Q_WIDTH = 1024
GQA_KV_WIDTH = 256
GQA_KV_HEADS = 4
GQA_REP = 4
GRID_W = 64
ROPE_THETA = 10000.0
N_EXPERTS = 256
TOP_K = 8
N_EXPERT_GROUPS = 8
GROUP_SIZE = N_EXPERTS // N_EXPERT_GROUPS
TOPK_GROUPS = 4
EXPERT_FF = 256
ROUTED_SCALE = 2.5
NORM_EPS = 1e-6
HALF = D_MODEL // 2

LANES = 128
VMEM_LIMIT = 56 * 1024 * 1024

TM_PROJ = 256
TM_MIX = 512
ZERO_CHUNK_ROWS = 2048
XS_RING = 3
TQ_GQA = 256
GQA_SUBTILES = 2
GQA_ITEM_REPS = 2
QB_DIL = 128
DIL_UNROLL = 2
TR_ROUTE = 512
TM_DISPATCH = 256
TM_COMBINE = 128
BM_EXPERT = 512

NEG_INF = float("-inf")


def _params(sem):
    return pltpu.CompilerParams(dimension_semantics=sem, vmem_limit_bytes=VMEM_LIMIT)


def _const_spec(shape, single_buffer=False):
    n = len(shape)
    if single_buffer:
        return pl.BlockSpec(shape, lambda *_: (0,) * n, pipeline_mode=pl.Buffered(1))
    return pl.BlockSpec(shape, lambda *_: (0,) * n)


def _adaln_kernel(c_ref, w_ref, b_ref, o_ref):
    c = c_ref[...]
    a = c * jax.nn.sigmoid(c)
    o_ref[...] = jnp.dot(a, w_ref[...], preferred_element_type=F32,
                         precision=lax.Precision.HIGHEST) + b_ref[...]


def _adaln(c, w_ada, b_ada):
    B = c.shape[0]
    return pl.pallas_call(
        _adaln_kernel,
        grid=(6,),
        in_specs=[pl.BlockSpec((B, D_MODEL), lambda j: (0, 0)),
                  pl.BlockSpec((D_MODEL, D_MODEL), lambda j: (0, j)),
                  pl.BlockSpec((1, D_MODEL), lambda j: (0, j))],
        out_specs=pl.BlockSpec((B, D_MODEL), lambda j: (0, j)),
        out_shape=jax.ShapeDtypeStruct((B, 6 * D_MODEL), F32),
        compiler_params=_params(("arbitrary",)),
        name="adaln",
    )(c, w_ada, b_ada.reshape(1, 6 * D_MODEL))


def _tile_lanes(t, width):
    return jnp.concatenate([t] * (width // LANES), axis=1)


def _rope(v, cos, sin_signed, half):
    w = v.shape[-1]
    lane = lax.broadcasted_iota(I32, (1, w), 1)
    first = (lane % (2 * half)) < half
    up = pltpu.roll(v, w - half, axis=1)
    dn = pltpu.roll(v, half, axis=1)
    return v * _tile_lanes(cos, w) + jnp.where(first, up, dn) * _tile_lanes(sin_signed, w)


def _head_rms(v, gain, bd_ref):
    w = v.shape[-1]
    parts = []
    for c0 in range(0, w, 256):
        vc = v[:, c0:c0 + 256]
        ms = jnp.dot((vc * vc).astype(BF16), bd_ref[...], preferred_element_type=F32) * (1.0 / HEAD_DIM)
        parts.append(vc * lax.rsqrt(ms + NORM_EPS))
    return jnp.concatenate(parts, axis=1) * gain


def _in_proj_kernel(x_ref, mod_ref, g1_ref, wqa_ref, wka_ref, wva_ref, wqb_ref, wkb_ref, wvb_ref,
                    wga_ref, wgb_ref, bg_ref, qn_ref, kn_ref, cos1_ref, sin1_ref, cosb_ref, sinb_ref,
                    bd_ref, qa_ref, ka_ref, va_ref, qb_ref, kbt_ref, vb_ref, ga_ref, gb_ref, zero_ref,
                    zbuf_ref, zsem, *, zero_chunks):
    step = pl.program_id(0) * pl.num_programs(1) + pl.program_id(1)

    @pl.when(step == 0)
    def _():
        zbuf_ref[...] = jnp.zeros_like(zbuf_ref)

    zr = zbuf_ref.shape[0]
    zero_copies = [pltpu.make_async_copy(zbuf_ref, zero_ref.at[pl.ds((step * zero_chunks + j) * zr, zr), :], zsem)
                   for j in range(zero_chunks)]
    for cp in zero_copies:
        cp.start()
    x = x_ref[0]
    ms = jnp.mean(x * x, axis=-1, keepdims=True)
    h = x * lax.rsqrt(ms + NORM_EPS) * g1_ref[...]
    h = h * (1.0 + mod_ref[0, 1:2, :]) + mod_ref[0, 0:1, :]
    hb = h.astype(BF16)

    def proj(w_ref):
        return jnp.dot(hb, w_ref[...], preferred_element_type=F32)

    cos1, sin1 = cos1_ref[...], sin1_ref[...]
    cosb, sinb = cosb_ref[...], sinb_ref[...]

    qa = _rope(proj(wqa_ref), cos1, sin1, HEAD_DIM // 2) * (HEAD_DIM ** -0.5)
    ka = _rope(proj(wka_ref), cos1, sin1, HEAD_DIM // 2)
    va = proj(wva_ref)
    for g in range(N_DIL_GROUPS):
        sl = slice(g * DIL_GROUP_WIDTH, (g + 1) * DIL_GROUP_WIDTH)
        qa_ref[g, 0] = _pack_halves(qa[:, sl])
        ka_ref[g, 0] = _pack_halves(ka[:, sl])
        va_ref[g, 0] = _pack_halves(va[:, sl])

    qb = _head_rms(proj(wqb_ref), qn_ref[...], bd_ref)
    qb_ref[0] = (_rope(qb, cosb, sinb, HEAD_DIM // 4) * (HEAD_DIM ** -0.5)).astype(BF16)
    kb = _rope(_head_rms(proj(wkb_ref), kn_ref[...], bd_ref), cosb, sinb, HEAD_DIM // 4)
    kbt_ref[0] = kb.T.astype(BF16)
    vb_ref[0] = proj(wvb_ref).astype(BF16)

    ga_ref[0] = jax.nn.sigmoid(proj(wga_ref) + bg_ref[:, :D_MODEL]).astype(BF16)
    gb_ref[0] = jax.nn.sigmoid(proj(wgb_ref) + bg_ref[:, D_MODEL:]).astype(BF16)
    for cp in zero_copies:
        cp.wait()


def _in_proj(x, mod, g1, weights, bgate, qn, kn, tables, bd, rows_padded):
    B, S, _ = x.shape
    tm = TM_PROJ
    steps = B * (S // tm)
    zero_rows = rows_padded * ROW_SUBLANES
    assert zero_rows % (steps * ZERO_CHUNK_ROWS) == 0
    zero_chunks = zero_rows // (steps * ZERO_CHUNK_ROWS)
    tok = lambda w: pl.BlockSpec((1, tm, w), lambda b, i: (b, i, 0))
    tab = pl.BlockSpec((tm, LANES), lambda b, i: (i, 0))
    dil_out = pl.BlockSpec((N_DIL_GROUPS, 1, tm, DIL_PACKED), lambda b, i: (0, b, i, 0))
    dil_shape = jax.ShapeDtypeStruct((N_DIL_GROUPS, B, S, DIL_PACKED), U32)
    in_specs = ([tok(D_MODEL), pl.BlockSpec((1, 6, D_MODEL), lambda b, i: (b, 0, 0)), _const_spec((1, D_MODEL))]
                + [_const_spec(w.shape, single_buffer=True) for w in weights]
                + [_const_spec(bgate.shape), _const_spec(qn.shape), _const_spec(kn.shape), tab, tab, tab, tab,
                   _const_spec(bd.shape)])
    out_specs = [dil_out, dil_out, dil_out, tok(GQA_Q_WIDTH),
                 pl.BlockSpec((1, GQA_KV_WIDTH, tm), lambda b, i: (b, 0, i)), tok(GQA_KV_WIDTH),
                 tok(D_MODEL), tok(D_MODEL), pl.BlockSpec(memory_space=pl.ANY)]
    out_shape = [dil_shape, dil_shape, dil_shape,
                 jax.ShapeDtypeStruct((B, S, GQA_Q_WIDTH), BF16),
                 jax.ShapeDtypeStruct((B, GQA_KV_WIDTH, S), BF16),
                 jax.ShapeDtypeStruct((B, S, GQA_KV_WIDTH), BF16),
                 jax.ShapeDtypeStruct((B, S, D_MODEL), BF16),
                 jax.ShapeDtypeStruct((B, S, D_MODEL), BF16),
                 jax.ShapeDtypeStruct((rows_padded * ROW_SUBLANES, LANES), U32)]
    return pl.pallas_call(
        functools.partial(_in_proj_kernel, zero_chunks=zero_chunks),
        grid=(B, S // tm),
        in_specs=in_specs,
        out_specs=out_specs,
        out_shape=out_shape,
        scratch_shapes=[pltpu.VMEM((ZERO_CHUNK_ROWS, LANES), U32), pltpu.SemaphoreType.DMA(())],
        compiler_params=_params(("arbitrary", "arbitrary")),
        name="in_proj",
    )(x, mod, g1, *weights, bgate, qn, kn, *tables, bd)


def _dil_kernel(q_ref, k_ref, v_ref, o_ref, lse_lo_ref, lse_hi_ref, *, dil, seq):
    n = DIL_HALF_WINDOW
    qb = min(QB_DIL, seq)
    kw = min(qb + 2 * n, seq)
    n_heads = DIL_GROUP_WIDTH // HEAD_DIM
    lane = lax.broadcasted_iota(I32, (1, DIL_GROUP_WIDTH), 1)
    head_masks = [(lane >= h * HEAD_DIM) & (lane < (h + 1) * HEAD_DIM) for h in range(n_heads)]

    def rows(ref, start, count, r):
        if dil == 1:
            return ref[pl.ds(start, count), :]
        return ref[pl.ds(start * dil + r, count, stride=dil), :]

    def load(ref, start, count, r):
        lo, hi = _unpack_halves(rows(ref, start, count, r))
        return jnp.concatenate([lo, hi], axis=1).astype(BF16)

    def block(i, r):
        a = pl.multiple_of(i * qb, qb)
        ws = pl.multiple_of(jnp.clip(a - n, 0, seq - kw), n)
        q, k, v = load(q_ref, a, qb, r), load(k_ref, ws, kw, r), load(v_ref, ws, kw, r)
        qs = jnp.concatenate([jnp.where(hm, q, jnp.zeros_like(q)) for hm in head_masks], axis=0)
        s = lax.dot_general(qs, k, (((1,), (1,)), ((), ())), preferred_element_type=F32)
        qpos = a + lax.broadcasted_iota(I32, (qb, 1), 0)
        kpos = ws + lax.broadcasted_iota(I32, (1, kw), 1)
        valid = jnp.abs(kpos - qpos) <= n
        s = jnp.where(jnp.concatenate([valid] * n_heads, axis=0), s, NEG_INF)
        m = jnp.max(s, axis=1, keepdims=True)
        p = jnp.exp(s - m)
        l = jnp.sum(p, axis=1, keepdims=True)
        pv = jnp.dot(p.astype(BF16), v, preferred_element_type=F32) * (1.0 / l)
        lse = m + jnp.log(l)
        o_acc = jnp.zeros((qb, DIL_GROUP_WIDTH), F32)
        lse_acc = jnp.zeros((qb, DIL_GROUP_WIDTH), F32)
        for h, hm in enumerate(head_masks):
            o_acc = jnp.where(hm, pv[h * qb:(h + 1) * qb], o_acc)
            lse_acc = jnp.where(hm, lse[h * qb:(h + 1) * qb], lse_acc)
        packed = _pack_halves(o_acc)
        if dil == 1:
            dst = pl.ds(a, qb)
        else:
            dst = pl.ds(a * dil + r, qb, stride=dil)
        o_ref[dst, :] = packed
        lse_lo_ref[dst, :] = lse_acc[:, :DIL_PACKED]
        lse_hi_ref[dst, :] = lse_acc[:, DIL_PACKED:]

    for r in range(dil):
        def body(i, carry, r=r):
            block(i, r)
            return carry
        lax.fori_loop(0, seq // qb, body, 0, unroll=min(DIL_UNROLL, seq // qb))


def _dilated(qa, ka, va, g):
    _, B, S, W = qa.shape
    dil = DIL_CONFIGS[g][1]
    in_spec = pl.BlockSpec((None, None, S, W), lambda b: (g, b, 0, 0))
    out_spec = pl.BlockSpec((None, S, W), lambda b: (b, 0, 0))
    return pl.pallas_call(
        functools.partial(_dil_kernel, dil=dil, seq=S // dil),
        grid=(B,),
        in_specs=[in_spec, in_spec, in_spec],
        out_specs=[out_spec, out_spec, out_spec],
        out_shape=[jax.ShapeDtypeStruct((B, S, W), U32), jax.ShapeDtypeStruct((B, S, W), F32),
                   jax.ShapeDtypeStruct((B, S, W), F32)],
        compiler_params=_params(("arbitrary",)),
        name=f"dilated{g}",
    )(qa, ka, va)


def _gqa_kernel(q_ref, kt_ref, v_ref, o_ref, s0_ref, s1_ref, p0_ref, p1_ref):
    tq = TQ_GQA
    s_refs, p_refs = (s0_ref, s1_ref), (p0_ref, p1_ref)
    lane = lax.broadcasted_iota(I32, (1, GQA_KV_WIDTH), 1)
    head_masks = [(lane >= g * HEAD_DIM) & (lane < (g + 1) * HEAD_DIM) for g in range(GQA_KV_HEADS)]
    nrep = GQA_ITEM_REPS
    items = [(sub, c0, g) for sub in range(q_ref.shape[0] // tq) for c0 in range(0, GQA_REP, nrep)
             for g in range(GQA_KV_HEADS)]

    def scores(i):
        sub, c0, g = items[i]
        rows = slice(sub * tq, (sub + 1) * tq)
        qg = jnp.concatenate(
            [jnp.where(head_masks[g], q_ref[rows, c * 256:(c + 1) * 256], jnp.zeros((tq, 256), BF16))
             for c in range(c0, c0 + nrep)], axis=0)
        s_refs[i % 2][...] = jnp.dot(qg, kt_ref[...], preferred_element_type=F32)

    scores(0)
    outs = None
    for i, (sub, c0, g) in enumerate(items):
        if i + 1 < len(items):
            scores(i + 1)
        if g == 0:
            outs = [jnp.zeros((tq, GQA_KV_WIDTH), F32) for _ in range(nrep)]
        s = s_refs[i % 2][...]
        m = jnp.max(s, axis=1, keepdims=True)
        e = jnp.exp(s - m)
        l = jnp.sum(e, axis=1, keepdims=True)
        p_refs[i % 2][...] = e.astype(BF16)
        pv = jnp.dot(p_refs[i % 2][...], v_ref[...], preferred_element_type=F32) * (1.0 / l)
        for j in range(nrep):
            outs[j] = jnp.where(head_masks[g], pv[j * tq:(j + 1) * tq], outs[j])
        if g == GQA_KV_HEADS - 1:
            rows = slice(sub * tq, (sub + 1) * tq)
            for j in range(nrep):
                o_ref[rows, (c0 + j) * 256:(c0 + j + 1) * 256] = outs[j].astype(BF16)


def _gqa(qb, kbt, vb):
    B, S, _ = qb.shape
    tq = TQ_GQA * GQA_SUBTILES
    item_rows = TQ_GQA * GQA_ITEM_REPS
    return pl.pallas_call(
        _gqa_kernel,
        grid=(B, S // tq),
        in_specs=[pl.BlockSpec((None, tq, GQA_Q_WIDTH), lambda b, i: (b, i, 0)),
                  pl.BlockSpec((None, GQA_KV_WIDTH, S), lambda b, i: (b, 0, 0)),
                  pl.BlockSpec((None, S, GQA_KV_WIDTH), lambda b, i: (b, 0, 0))],
        out_specs=pl.BlockSpec((None, tq, GQA_Q_WIDTH), lambda b, i: (b, i, 0)),
        out_shape=jax.ShapeDtypeStruct((B, S, GQA_Q_WIDTH), BF16),
        scratch_shapes=[pltpu.VMEM((item_rows, S), F32), pltpu.VMEM((item_rows, S), F32),
                        pltpu.VMEM((item_rows, S), BF16), pltpu.VMEM((item_rows, S), BF16)],
        compiler_params=_params(("arbitrary", "arbitrary")),
        name="gqa",
    )(qb, kbt, vb)


def _pack_halves(y):
    half = y.shape[1] // 2
    lo = lax.bitcast_convert_type(y[:, :half].astype(BF16).astype(F32), U32)
    hi = lax.bitcast_convert_type(y[:, half:].astype(BF16).astype(F32), U32)
    return (lo >> 16) | hi


def _unpack_halves(u):
    lo = lax.bitcast_convert_type(u << 16, F32)
    hi = lax.bitcast_convert_type(u & jnp.uint32(0xFFFF0000), F32)
    return lo, hi


ROW_SUBLANES = HALF // LANES


def _store_rows(ref, packed):
    m = packed.shape[0]
    for c in range(ROW_SUBLANES):
        ref[pl.ds(c, m, stride=ROW_SUBLANES), :] = packed[:, c * LANES:(c + 1) * LANES]


def _load_rows(ref, m):
    return jnp.concatenate([ref[pl.ds(c, m, stride=ROW_SUBLANES), :] for c in range(ROW_SUBLANES)], axis=1)


def _mix_kernel(x_ref, *refs):
    dil_refs, refs = refs[:3 * N_DIL_GROUPS], refs[3 * N_DIL_GROUPS:]
    (yb_ref, ga_ref, gb_ref, mod_ref, wod_ref, wog_ref, wout_ref, g2_ref, rw_ref, wsg_ref, wsu_ref, wsd_ref,
     xs_ref, h2_ref, logit_ref) = refs

    outs = [_unpack_halves(dil_refs[3 * g][0]) for g in range(N_DIL_GROUPS)]
    halves = []
    for part in range(2):
        lses = [dil_refs[3 * g + 1 + part][0] for g in range(N_DIL_GROUPS)]
        mx = functools.reduce(jnp.maximum, lses)
        es = [jnp.exp(l - mx) for l in lses]
        num = sum(e * outs[g][part] for g, e in enumerate(es))
        halves.append(num / sum(es))
    ya = jnp.concatenate(halves, axis=1)
    y_a = jnp.dot(ya.astype(BF16), wod_ref[...], preferred_element_type=F32)
    y_b = jnp.dot(yb_ref[0], wog_ref[...], preferred_element_type=F32)
    mix = ga_ref[0].astype(F32) * y_a + gb_ref[0].astype(F32) * y_b
    mixed = jnp.dot(mix.astype(BF16), wout_ref[...], preferred_element_type=F32)
    x1 = x_ref[0] + mod_ref[0, 2:3, :] * mixed

    ms = jnp.mean(x1 * x1, axis=-1, keepdims=True)
    h2 = x1 * lax.rsqrt(ms + NORM_EPS) * g2_ref[...]
    h2 = h2 * (1.0 + mod_ref[0, 4:5, :]) + mod_ref[0, 3:4, :]
    h2b = h2.astype(BF16)
    _store_rows(h2_ref, _pack_halves(h2))
    logit_ref[0] = jnp.dot(h2b, rw_ref[...], preferred_element_type=F32)

    gt = jnp.dot(h2b, wsg_ref[...], preferred_element_type=F32)
    up = jnp.dot(h2b, wsu_ref[...], preferred_element_type=F32)
    act = (gt * jax.nn.sigmoid(gt) * up).astype(BF16)
    shared = jnp.dot(act, wsd_ref[...], preferred_element_type=F32)
    xs_ref[0] = x1 + mod_ref[0, 5:6, :] * shared


def _mix(x, dil_outs, yb, ga, gb, mod, weights):
    B, S, _ = x.shape
    tm = TM_MIX
    tok = lambda w: pl.BlockSpec((1, tm, w), lambda b, i: (b, i, 0))
    dil_flat = [a for group in dil_outs for a in group]
    in_specs = ([tok(D_MODEL)] + [tok(DIL_PACKED)] * len(dil_flat) + [tok(GQA_Q_WIDTH), tok(D_MODEL), tok(D_MODEL),
                pl.BlockSpec((1, 6, D_MODEL), lambda b, i: (b, 0, 0))] + [_const_spec(w.shape) for w in weights])
    return pl.pallas_call(
        _mix_kernel,
        grid=(B, S // tm),
        in_specs=in_specs,
        out_specs=[tok(D_MODEL),
                   pl.BlockSpec((tm * ROW_SUBLANES, LANES), lambda b, i: (b * (S // tm) + i, 0)),
                   tok(N_EXPERTS)],
        out_shape=[jax.ShapeDtypeStruct((B, S, D_MODEL), F32),
                   jax.ShapeDtypeStruct((B * S * ROW_SUBLANES, LANES), U32),
                   jax.ShapeDtypeStruct((B, S, N_EXPERTS), F32)],
        compiler_params=_params(("arbitrary", "arbitrary")),
        name="mix",
    )(x, *dil_flat, yb, ga, gb, mod, *weights)


def _first_argmax(v, idx, big):
    m = jnp.max(v, axis=0, keepdims=True)
    i = jnp.min(jnp.where(v == m, idx, big), axis=0, keepdims=True)
    return m, i


def _route_kernel(logit_ref, bias_ref, tri_ref, idx_ref, gate_ref, gate_tm_ref, rank_ref, cnt_ref, base_ref):
    step = pl.program_id(0)
    tr = logit_ref.shape[0]

    @pl.when(step == 0)
    def _():
        base_ref[...] = jnp.zeros_like(base_ref)

    scores = jax.nn.sigmoid(logit_ref[...].T)
    biased = scores + bias_ref[...]
    row = lax.broadcasted_iota(I32, (N_EXPERTS, tr), 0)

    gscore = []
    group_row = lax.broadcasted_iota(I32, (GROUP_SIZE, tr), 0)
    for g in range(N_EXPERT_GROUPS):
        v, ri = biased[g * GROUP_SIZE:(g + 1) * GROUP_SIZE], group_row + g * GROUP_SIZE
        m1, i1 = _first_argmax(v, ri, N_EXPERTS)
        m2 = jnp.max(jnp.where(ri == i1, NEG_INF, v), axis=0, keepdims=True)
        gscore.append(m1 + m2)
    cur = jnp.concatenate(gscore, axis=0)
    gi = lax.broadcasted_iota(I32, (N_EXPERT_GROUPS, tr), 0)
    sel = jnp.zeros((N_EXPERT_GROUPS, tr), F32)
    for _ in range(TOPK_GROUPS):
        _, i = _first_argmax(cur, gi, N_EXPERT_GROUPS)
        hit = gi == i
        sel = jnp.where(hit, 1.0, sel)
        cur = jnp.where(hit, NEG_INF, cur)

    cur = jnp.concatenate(
        [jnp.where(sel[g:g + 1] > 0, biased[g * GROUP_SIZE:(g + 1) * GROUP_SIZE], NEG_INF)
         for g in range(N_EXPERT_GROUPS)], axis=0)
    idxs, gates, hits = [], [], []
    assigned = jnp.zeros((N_EXPERTS, tr), F32)
    for _ in range(TOP_K):
        _, i = _first_argmax(cur, row, N_EXPERTS)
        hit = row == i
        gates.append(jnp.sum(jnp.where(hit, scores, 0.0), axis=0, keepdims=True))
        cur = jnp.where(hit, NEG_INF, cur)
        assigned = jnp.where(hit, 1.0, assigned)
        idxs.append(i)
        hits.append(hit)
    gate = jnp.concatenate(gates, axis=0)
    gate = gate / jnp.sum(gate, axis=0, keepdims=True) * ROUTED_SCALE

    before = jnp.dot(assigned.astype(BF16), tri_ref[...], preferred_element_type=F32) + base_ref[:, 0:1]
    ranks = [jnp.sum(jnp.where(h, before, 0.0), axis=0, keepdims=True) for h in hits]

    idx_ref[...] = jnp.concatenate(idxs, axis=0)
    gate_ref[...] = gate
    rank_ref[...] = jnp.concatenate(ranks, axis=0).astype(I32)
    gate_tm_ref[...] = jnp.concatenate([gate, jnp.zeros((LANES - TOP_K, tr), F32)], axis=0).T
    base_ref[...] = base_ref[...] + jnp.sum(assigned, axis=1, keepdims=True)
    cnt_ref[...] = base_ref[...]


def _route(logits, bias):
    T = logits.shape[0]
    tr = TR_ROUTE
    tri = (lax.broadcasted_iota(I32, (tr, tr), 0) < lax.broadcasted_iota(I32, (tr, tr), 1)).astype(BF16)
    slot = pl.BlockSpec((TOP_K, tr), lambda i: (0, i))
    return pl.pallas_call(
        _route_kernel,
        grid=(T // tr,),
        in_specs=[pl.BlockSpec((tr, N_EXPERTS), lambda i: (i, 0)), _const_spec((N_EXPERTS, 1)), _const_spec((tr, tr))],
        out_specs=[slot, slot, pl.BlockSpec((tr, LANES), lambda i: (i, 0)), slot, _const_spec((N_EXPERTS, LANES))],
        out_shape=[jax.ShapeDtypeStruct((TOP_K, T), I32), jax.ShapeDtypeStruct((TOP_K, T), F32),
                   jax.ShapeDtypeStruct((T, LANES), F32), jax.ShapeDtypeStruct((TOP_K, T), I32),
                   jax.ShapeDtypeStruct((N_EXPERTS, LANES), F32)],
        scratch_shapes=[pltpu.VMEM((N_EXPERTS, LANES), F32)],
        compiler_params=_params(("arbitrary",)),
        name="route",
    )(logits, bias.reshape(N_EXPERTS, 1), tri)


def _plan_kernel(cnt_ref, idx_ref, rank_ref, ltri_ref, dest_ref, bexp_ref, nblk_ref, *, nb_max):
    tr = idx_ref.shape[1]
    blocks = jnp.floor((cnt_ref[...] + (BM_EXPERT - 1)) * (1.0 / BM_EXPERT))
    start = jnp.dot(ltri_ref[...], blocks.astype(BF16), preferred_element_type=F32)
    end = start + blocks
    blk = lax.broadcasted_iota(I32, (1, nb_max), 1).astype(F32)
    bexp = jnp.sum(jnp.where(end[:, 0:1] <= blk, 1.0, 0.0), axis=0, keepdims=True)
    bexp_ref[...] = jnp.minimum(bexp, N_EXPERTS - 1.0).astype(I32)
    nblk_ref[...] = jnp.sum(blocks, axis=0, keepdims=True).astype(I32)

    row = lax.broadcasted_iota(I32, (N_EXPERTS, tr), 0)
    start_rows = start[:, 0:1] * float(BM_EXPERT)
    dests = []
    for k in range(TOP_K):
        hit = row == idx_ref[k:k + 1, :]
        dests.append(jnp.sum(jnp.where(hit, start_rows, 0.0), axis=0, keepdims=True))
    dest = jnp.concatenate(dests, axis=0) + rank_ref[...].astype(F32)
    dest_ref[...] = jnp.concatenate([dest, jnp.zeros((LANES - TOP_K, tr), F32)], axis=0).T.astype(I32)


def _plan(counts, idx, rank, nb_max):
    T = idx.shape[1]
    tr = TR_ROUTE
    ltri = (lax.broadcasted_iota(I32, (N_EXPERTS, N_EXPERTS), 1)
            < lax.broadcasted_iota(I32, (N_EXPERTS, N_EXPERTS), 0)).astype(BF16)
    slot = pl.BlockSpec((TOP_K, tr), lambda i: (0, i))
    return pl.pallas_call(
        functools.partial(_plan_kernel, nb_max=nb_max),
        grid=(T // tr,),
        in_specs=[_const_spec((N_EXPERTS, LANES)), slot, slot, _const_spec((N_EXPERTS, N_EXPERTS))],
        out_specs=[pl.BlockSpec((tr, LANES), lambda i: (i, 0)), _const_spec((1, nb_max)), _const_spec((1, LANES))],
        out_shape=[jax.ShapeDtypeStruct((T, LANES), I32), jax.ShapeDtypeStruct((1, nb_max), I32),
                   jax.ShapeDtypeStruct((1, LANES), I32)],
        compiler_params=_params(("arbitrary",)),
        name="plan",
    )(counts, idx, rank, ltri)


def _dispatch_kernel(dest_ref, h_ref, xs_in_ref, xs_ref, sem):
    del xs_in_ref
    tm = h_ref.shape[0]

    def issue(t, carry):
        for k in range(TOP_K):
            pltpu.make_async_copy(h_ref.at[t], xs_ref.at[dest_ref[t * TOP_K + k]], sem).start(priority=k % 2)
        return carry

    lax.fori_loop(0, tm, issue, 0)
    for _ in range(TOP_K):
        pltpu.make_async_copy(h_ref, xs_ref.at[pl.ds(0, tm)], sem).wait()


def _dispatch(dest_flat, h_rows, xs0):
    T = h_rows.shape[0]
    rows_padded = xs0.shape[0]
    tm = TM_DISPATCH
    return pl.pallas_call(
        _dispatch_kernel,
        grid=(T // tm,),
        in_specs=[pl.BlockSpec((tm * TOP_K,), lambda i: (i,), memory_space=pltpu.SMEM),
                  pl.BlockSpec((tm, ROW_SUBLANES, LANES), lambda i: (i, 0, 0)),
                  pl.BlockSpec(memory_space=pl.ANY)],
        out_specs=pl.BlockSpec(memory_space=pl.ANY),
        out_shape=jax.ShapeDtypeStruct((rows_padded, ROW_SUBLANES, LANES), U32),
        scratch_shapes=[pltpu.SemaphoreType.DMA(())],
        input_output_aliases={2: 0},
        compiler_params=_params(("arbitrary",)),
        name="dispatch",
    )(dest_flat, h_rows, xs0)


def _expert_kernel(bexp_ref, nblk_ref, xs_ref, wg_ref, wu_ref, wd_ref, ys_ref, wgb_ref, wub_ref, wdb_ref,
                   xbuf_ref, xsem, obuf_ref, osem):
    p = pl.program_id(0)
    nb = nblk_ref[0]
    prev = bexp_ref[jnp.maximum(p - 1, 0)]
    active = p < nb
    block_rows = BM_EXPERT * ROW_SUBLANES

    def block(ref, q):
        return ref.at[pl.ds(pl.multiple_of(q * block_rows, block_rows), block_rows), :]

    def fetch(q):
        return pltpu.make_async_copy(block(xs_ref, q), xbuf_ref.at[q % XS_RING], xsem.at[q % XS_RING])

    def writeback(q):
        return pltpu.make_async_copy(obuf_ref.at[q % 2], block(ys_ref, q), osem.at[q % 2])

    @pl.when(p == 0)
    def _():
        fetch(0).start()

        @pl.when(nb > 1)
        def _():
            fetch(1).start()

    @pl.when(p + 2 < nb)
    def _():
        fetch(p + 2).start()

    @pl.when(active & ((p == 0) | (bexp_ref[p] != prev)))
    def _():
        wgb_ref[...] = wg_ref[...].astype(BF16)
        wub_ref[...] = wu_ref[...].astype(BF16)
        wdb_ref[...] = wd_ref[...].astype(BF16)

    @pl.when(active)
    def _():
        fetch(p).wait()
        lo, hi = _unpack_halves(_load_rows(xbuf_ref.at[p % XS_RING], BM_EXPERT))
        lo, hi = lo.astype(BF16), hi.astype(BF16)

        def proj(w_ref):
            return (jnp.dot(lo, w_ref[:HALF, :], preferred_element_type=F32)
                    + jnp.dot(hi, w_ref[HALF:, :], preferred_element_type=F32))

        gt, up = proj(wgb_ref), proj(wub_ref)
        act = (gt * jax.nn.sigmoid(gt) * up).astype(BF16)
        _store_rows(obuf_ref.at[p % 2], _pack_halves(jnp.dot(act, wdb_ref[...], preferred_element_type=F32)))
        writeback(p).start()

        @pl.when(p > 0)
        def _():
            writeback(p - 1).wait()

        @pl.when(p == nb - 1)
        def _():
            writeback(p).wait()


def _experts(bexp, nblk, xs, w_gate, w_up, w_down, nb_max):
    bm = BM_EXPERT
    wspec = lambda shape: pl.BlockSpec((None,) + shape, lambda p, be, nb: (be[p], 0, 0))
    block_shape = (bm * ROW_SUBLANES, LANES)
    grid_spec = pltpu.PrefetchScalarGridSpec(
        num_scalar_prefetch=2,
        grid=(nb_max,),
        in_specs=[pl.BlockSpec(memory_space=pl.ANY), wspec((D_MODEL, EXPERT_FF)),
                  wspec((D_MODEL, EXPERT_FF)), wspec((EXPERT_FF, D_MODEL))],
        out_specs=pl.BlockSpec(memory_space=pl.ANY),
        scratch_shapes=[pltpu.VMEM((D_MODEL, EXPERT_FF), BF16), pltpu.VMEM((D_MODEL, EXPERT_FF), BF16),
                        pltpu.VMEM((EXPERT_FF, D_MODEL), BF16),
                        pltpu.VMEM((XS_RING,) + block_shape, U32), pltpu.SemaphoreType.DMA((XS_RING,)),
                        pltpu.VMEM((2,) + block_shape, U32), pltpu.SemaphoreType.DMA((2,))],
    )
    return pl.pallas_call(
        _expert_kernel,
        grid_spec=grid_spec,
        out_shape=jax.ShapeDtypeStruct(xs.shape, U32),
        input_output_aliases={2: 0},
        compiler_params=_params(("arbitrary",)),
        name="experts",
    )(bexp, nblk, xs, w_gate, w_up, w_down)


def _combine_kernel(dest_ref, dnext_ref, xs_ref, gate_ref, mod_ref, fg_ref, ys_ref, ys2d_ref, o_ref, buf_ref, sem):
    i = pl.program_id(0)
    n = pl.num_programs(0)
    tm = o_ref.shape[0]
    slot = i % 2

    def row_copy(d_ref, s, t, k):
        return pltpu.make_async_copy(ys_ref.at[d_ref[t * TOP_K + k]],
                                     buf_ref.at[s, k, pl.ds(t * ROW_SUBLANES, ROW_SUBLANES), :], sem.at[s])

    @pl.when(i == 0)
    def _():
        def body(t, carry):
            for k in range(TOP_K):
                row_copy(dest_ref, 0, t, k).start(priority=k % 2)
            return carry
        lax.fori_loop(0, tm, body, 0)

    def finish(prefetch):
        for k in range(TOP_K):
            pltpu.make_async_copy(ys2d_ref.at[pl.ds(0, tm * ROW_SUBLANES), :], buf_ref.at[slot, k],
                                  sem.at[slot]).wait()
        acc_lo = jnp.zeros((tm, HALF), F32)
        acc_hi = jnp.zeros((tm, HALF), F32)
        batch = tm // TOP_K
        for k in range(TOP_K):
            if prefetch:
                for t in range(k * batch, (k + 1) * batch):
                    for kk in range(TOP_K):
                        row_copy(dnext_ref, 1 - slot, t, kk).start(priority=kk % 2)
            lo, hi = _unpack_halves(_load_rows(buf_ref.at[slot, k], tm))
            g = gate_ref[:, k:k + 1]
            acc_lo = acc_lo + g * lo
            acc_hi = acc_hi + g * hi
        routed = jnp.concatenate([acc_lo, acc_hi], axis=1)
        x2 = xs_ref[...] + mod_ref[0, 5:6, :] * routed
        ms = jnp.mean(x2 * x2, axis=-1, keepdims=True)
        o_ref[...] = x2 * lax.rsqrt(ms + NORM_EPS) * fg_ref[...]

    @pl.when(i + 1 < n)
    def _():
        finish(True)

    @pl.when(i + 1 >= n)
    def _():
        finish(False)


def _combine(dest_flat, xs_mid, gate_tm, mod, final_g, ys2d, seq):
    T = xs_mid.shape[0]
    tm = TM_COMBINE
    n = T // tm
    per_seq = seq // tm
    ys_rows = ys2d.reshape(-1, ROW_SUBLANES, LANES)
    return pl.pallas_call(
        _combine_kernel,
        grid=(n,),
        in_specs=[pl.BlockSpec((tm * TOP_K,), lambda i: (i,), memory_space=pltpu.SMEM),
                  pl.BlockSpec((tm * TOP_K,), lambda i: (jnp.minimum(i + 1, n - 1),), memory_space=pltpu.SMEM),
                  pl.BlockSpec((tm, D_MODEL), lambda i: (i, 0)),
                  pl.BlockSpec((tm, LANES), lambda i: (i, 0)),
                  pl.BlockSpec((1, 6, D_MODEL), lambda i: (i // per_seq, 0, 0)),
                  _const_spec((1, D_MODEL)),
                  pl.BlockSpec(memory_space=pl.ANY),
                  pl.BlockSpec(memory_space=pl.ANY)],
        out_specs=pl.BlockSpec((tm, D_MODEL), lambda i: (i, 0)),
        out_shape=jax.ShapeDtypeStruct((T, D_MODEL), F32),
        scratch_shapes=[pltpu.VMEM((2, TOP_K, tm * ROW_SUBLANES, LANES), U32), pltpu.SemaphoreType.DMA((2,))],
        compiler_params=_params(("arbitrary",)),
        name="combine",
    )(dest_flat, dest_flat, xs_mid, gate_tm, mod, final_g, ys_rows, ys2d)


def _rope_tables(seq):
    def cos_sin(pos, dim):
        inv = ROPE_THETA ** (-jnp.arange(0, dim, 2, dtype=F32) / dim)
        ang = pos.astype(F32)[:, None] * inv[None, :]
        ang = jnp.concatenate([ang, ang], axis=-1)
        sign = jnp.where(jnp.arange(dim) < dim // 2, -1.0, 1.0).astype(F32)
        return jnp.cos(ang), jnp.sin(ang) * sign

    t = jnp.arange(seq, dtype=I32)
    cos1, sin1 = cos_sin(t, HEAD_DIM)
    cos_r, sin_r = cos_sin(t // GRID_W, HEAD_DIM // 2)
    cos_c, sin_c = cos_sin(t % GRID_W, HEAD_DIM // 2)
    cosb = jnp.concatenate([cos_r, cos_c], axis=-1)
    sinb = jnp.concatenate([sin_r, sin_c], axis=-1)
    rep = LANES // HEAD_DIM
    return tuple(jnp.tile(a, (1, rep)) for a in (cos1, sin1, cosb, sinb))


def kernel(x, c, w_ada, b_ada, norm1_g, w_in, b_gate, qn_g, kn_g, w_o_dil, w_o_gqa, w_out, norm2_g,
           router_w, router_bias, w_exp_gate, w_exp_up, w_exp_down, w_sh_gate, w_sh_up, w_sh_down, final_g):
    B, S, D = x.shape
    T = B * S
    assert D == D_MODEL and S % (TM_PROJ * 1) == 0 and S // 16 >= QB_DIL
    row = lambda v: v.reshape(1, -1)

    mod = _adaln(c, w_ada, b_ada).reshape(B, 6, D)

    wb = w_in.astype(BF16)
    offs = [0, DIL_WIDTH, 2 * DIL_WIDTH, 3 * DIL_WIDTH, 3 * DIL_WIDTH + GQA_Q_WIDTH,
            3 * DIL_WIDTH + GQA_Q_WIDTH + GQA_KV_WIDTH, 3 * DIL_WIDTH + GQA_Q_WIDTH + 2 * GQA_KV_WIDTH]
    w_qa, w_ka, w_va = (wb[:, offs[i]:offs[i + 1]] for i in range(3))
    w_qb = wb[:, offs[3]:offs[4]].reshape(D, GQA_KV_HEADS, GQA_REP, HEAD_DIM).transpose(0, 2, 1, 3).reshape(D, GQA_Q_WIDTH)
    w_kb, w_vb = wb[:, offs[4]:offs[5]], wb[:, offs[5]:offs[6]]
    w_ga, w_gb = wb[:, offs[6]:offs[6] + D], wb[:, offs[6] + D:offs[6] + 2 * D]
    w_og = w_o_gqa.reshape(GQA_KV_HEADS, GQA_REP, HEAD_DIM, D).transpose(1, 0, 2, 3).reshape(GQA_Q_WIDTH, D).astype(BF16)

    lane_head = lax.broadcasted_iota(I32, (256, 256), 0) // HEAD_DIM
    bd = (lane_head == lane_head.T).astype(BF16)
    qn = jnp.tile(row(qn_g), (1, GQA_Q_WIDTH // HEAD_DIM))
    kn = jnp.tile(row(kn_g), (1, GQA_KV_WIDTH // HEAD_DIM))

    nb_max = -(-(T * TOP_K) // BM_EXPERT) + N_EXPERTS
    nb_max = -(-nb_max // LANES) * LANES
    rows_padded = nb_max * BM_EXPERT

    qa, ka, va, qb, kbt, vb, ga, gb, xs0 = _in_proj(
        x, mod, row(norm1_g), (w_qa, w_ka, w_va, w_qb, w_kb, w_vb, w_ga, w_gb), row(b_gate), qn, kn,
        _rope_tables(S), bd, rows_padded)

    dil_outs = [_dilated(qa, ka, va, g) for g in range(N_DIL_GROUPS)]
    yb = _gqa(qb, kbt, vb)

    xs_mid, h2p, logits = _mix(
        x, dil_outs, yb, ga, gb, mod,
        (w_o_dil.astype(BF16), w_og, w_out.astype(BF16), row(norm2_g), router_w.astype(BF16),
         w_sh_gate.astype(BF16), w_sh_up.astype(BF16), w_sh_down.astype(BF16)))

    idx, gate, gate_tm, rank, counts = _route(logits.reshape(T, N_EXPERTS), router_bias)
    del gate
    dest_tm, bexp, nblk = _plan(counts, idx, rank, nb_max)
    dest_flat = dest_tm[:, :TOP_K].reshape(T * TOP_K)

    xs = _dispatch(dest_flat, h2p.reshape(T, ROW_SUBLANES, LANES), xs0.reshape(rows_padded, ROW_SUBLANES, LANES))
    ys = _experts(bexp.reshape(nb_max), nblk.reshape(LANES)[:1], xs.reshape(rows_padded * ROW_SUBLANES, LANES),
                  w_exp_gate, w_exp_up, w_exp_down, nb_max)
    out = _combine(dest_flat, xs_mid.reshape(T, D), gate_tm, mod, row(final_g), ys, S)
    return out.reshape(B, S, D)
```

```python
import functools

import jax
import jax.numpy as jnp
from jax import lax
from jax.experimental import pallas as pl
from jax.experimental.pallas import tpu as pltpu

F32 = jnp.float32
BF16 = jnp.bfloat16
U32 = jnp.uint32
I32 = jnp.int32

D_MODEL = 1024
HEAD_DIM = 64
DIL_CONFIGS = ((128, 1), (512, 4), (2048, 16))
N_DIL_GROUPS = 3
DIL_GROUP_WIDTH = 256
DIL_WIDTH = N_DIL_GROUPS * DIL_GROUP_WIDTH
DIL_PACKED = DIL_GROUP_WIDTH // 2
DIL_HALF_WINDOW = 64
GQA_Q_WIDTH = 1024
GQA_KV_WIDTH = 256
GQA_KV_HEADS = 4
GQA_REP = 4
GRID_W = 64
ROPE_THETA = 10000.0
N_EXPERTS = 256
TOP_K = 8
N_EXPERT_GROUPS = 8
GROUP_SIZE = N_EXPERTS // N_EXPERT_GROUPS
TOPK_GROUPS = 4
EXPERT_FF = 256
ROUTED_SCALE = 2.5
NORM_EPS = 1e-6
HALF = D_MODEL // 2

LANES = 128
VMEM_LIMIT = 56 * 1024 * 1024

TM_PROJ = 256
TM_MIX = 512
ZERO_CHUNK_ROWS = 2048
XS_RING = 3
TQ_GQA = 256
GQA_SUBTILES = 2
GQA_ITEM_REPS = 2
QB_DIL = 128
DIL_UNROLL = 2
TR_ROUTE = 512
TM_DISPATCH = 1024
TM_COMBINE = 128
BM_EXPERT = 512

NEG_INF = float("-inf")


def _params(sem):
    return pltpu.CompilerParams(dimension_semantics=sem, vmem_limit_bytes=VMEM_LIMIT)


def _const_spec(shape, single_buffer=False):
    n = len(shape)
    if single_buffer:
        return pl.BlockSpec(shape, lambda *_: (0,) * n, pipeline_mode=pl.Buffered(1))
    return pl.BlockSpec(shape, lambda *_: (0,) * n)


def _adaln_kernel(c_ref, w_ref, b_ref, o_ref):
    c = c_ref[...]
    a = c * jax.nn.sigmoid(c)
    o_ref[...] = jnp.dot(a, w_ref[...], preferred_element_type=F32,
                         precision=lax.Precision.HIGHEST) + b_ref[...]


def _adaln(c, w_ada, b_ada):
    B = c.shape[0]
    return pl.pallas_call(
        _adaln_kernel,
        grid=(6,),
        in_specs=[pl.BlockSpec((B, D_MODEL), lambda j: (0, 0)),
                  pl.BlockSpec((D_MODEL, D_MODEL), lambda j: (0, j)),
                  pl.BlockSpec((1, D_MODEL), lambda j: (0, j))],
        out_specs=pl.BlockSpec((B, D_MODEL), lambda j: (0, j)),
        out_shape=jax.ShapeDtypeStruct((B, 6 * D_MODEL), F32),
        compiler_params=_params(("arbitrary",)),
        name="adaln",
    )(c, w_ada, b_ada.reshape(1, 6 * D_MODEL))


def _tile_lanes(t, width):
    return jnp.concatenate([t] * (width // LANES), axis=1)


def _rope(v, cos, sin_signed, half):
    w = v.shape[-1]
    lane = lax.broadcasted_iota(I32, (1, w), 1)
    first = (lane % (2 * half)) < half
    up = pltpu.roll(v, w - half, axis=1)
    dn = pltpu.roll(v, half, axis=1)
    return v * _tile_lanes(cos, w) + jnp.where(first, up, dn) * _tile_lanes(sin_signed, w)


def _head_rms(v, gain, bd_ref):
    w = v.shape[-1]
    parts = []
    for c0 in range(0, w, 256):
        vc = v[:, c0:c0 + 256]
        ms = jnp.dot((vc * vc).astype(BF16), bd_ref[...], preferred_element_type=F32) * (1.0 / HEAD_DIM)
        parts.append(vc * lax.rsqrt(ms + NORM_EPS))
    return jnp.concatenate(parts, axis=1) * gain


def _in_proj_kernel(x_ref, mod_ref, g1_ref, wqa_ref, wka_ref, wva_ref, wqb_ref, wkb_ref, wvb_ref,
                    wga_ref, wgb_ref, bg_ref, qn_ref, kn_ref, cos1_ref, sin1_ref, cosb_ref, sinb_ref,
                    bd_ref, qa_ref, ka_ref, va_ref, qb_ref, kbt_ref, vb_ref, ga_ref, gb_ref):
    x = x_ref[0]
    ms = jnp.mean(x * x, axis=-1, keepdims=True)
    h = x * lax.rsqrt(ms + NORM_EPS) * g1_ref[...]
    h = h * (1.0 + mod_ref[0, 1:2, :]) + mod_ref[0, 0:1, :]
    hb = h.astype(BF16)

    def proj(w_ref):
        return jnp.dot(hb, w_ref[...], preferred_element_type=F32)

    cos1, sin1 = cos1_ref[...], sin1_ref[...]
    cosb, sinb = cosb_ref[...], sinb_ref[...]

    qa = _rope(proj(wqa_ref), cos1, sin1, HEAD_DIM // 2) * (HEAD_DIM ** -0.5)
    ka = _rope(proj(wka_ref), cos1, sin1, HEAD_DIM // 2)
    va = proj(wva_ref)
    for g in range(N_DIL_GROUPS):
        sl = slice(g * DIL_GROUP_WIDTH, (g + 1) * DIL_GROUP_WIDTH)
        qa_ref[g, 0] = _pack_halves(qa[:, sl])
        ka_ref[g, 0] = _pack_halves(ka[:, sl])
        va_ref[g, 0] = _pack_halves(va[:, sl])

    qb = _head_rms(proj(wqb_ref), qn_ref[...], bd_ref)
    qb_ref[0] = (_rope(qb, cosb, sinb, HEAD_DIM // 4) * (HEAD_DIM ** -0.5)).astype(BF16)
    kb = _rope(_head_rms(proj(wkb_ref), kn_ref[...], bd_ref), cosb, sinb, HEAD_DIM // 4)
    kbt_ref[0] = kb.T.astype(BF16)
    vb_ref[0] = proj(wvb_ref).astype(BF16)

    ga_ref[0] = jax.nn.sigmoid(proj(wga_ref) + bg_ref[:, :D_MODEL]).astype(BF16)
    gb_ref[0] = jax.nn.sigmoid(proj(wgb_ref) + bg_ref[:, D_MODEL:]).astype(BF16)


def _in_proj(x, mod, g1, weights, bgate, qn, kn, tables, bd):
    B, S, _ = x.shape
    tm = TM_PROJ
    tok = lambda w: pl.BlockSpec((1, tm, w), lambda b, i: (b, i, 0))
    tab = pl.BlockSpec((tm, LANES), lambda b, i: (i, 0))
    dil_out = pl.BlockSpec((N_DIL_GROUPS, 1, tm, DIL_PACKED), lambda b, i: (0, b, i, 0))
    dil_shape = jax.ShapeDtypeStruct((N_DIL_GROUPS, B, S, DIL_PACKED), U32)
    in_specs = ([tok(D_MODEL), pl.BlockSpec((1, 6, D_MODEL), lambda b, i: (b, 0, 0)), _const_spec((1, D_MODEL))]
                + [_const_spec(w.shape, single_buffer=True) for w in weights]
                + [_const_spec(bgate.shape), _const_spec(qn.shape), _const_spec(kn.shape), tab, tab, tab, tab,
                   _const_spec(bd.shape)])
    out_specs = [dil_out, dil_out, dil_out, tok(GQA_Q_WIDTH),
                 pl.BlockSpec((1, GQA_KV_WIDTH, tm), lambda b, i: (b, 0, i)), tok(GQA_KV_WIDTH),
                 tok(D_MODEL), tok(D_MODEL)]
    out_shape = [dil_shape, dil_shape, dil_shape,
                 jax.ShapeDtypeStruct((B, S, GQA_Q_WIDTH), BF16),
                 jax.ShapeDtypeStruct((B, GQA_KV_WIDTH, S), BF16),
                 jax.ShapeDtypeStruct((B, S, GQA_KV_WIDTH), BF16),
                 jax.ShapeDtypeStruct((B, S, D_MODEL), BF16),
                 jax.ShapeDtypeStruct((B, S, D_MODEL), BF16)]
    return pl.pallas_call(
        _in_proj_kernel,
        grid=(B, S // tm),
        in_specs=in_specs,
        out_specs=out_specs,
        out_shape=out_shape,
        compiler_params=_params(("arbitrary", "arbitrary")),
        name="in_proj",
    )(x, mod, g1, *weights, bgate, qn, kn, *tables, bd)


def _dil_kernel(q_ref, k_ref, v_ref, o_ref, lse_lo_ref, lse_hi_ref, *, dil, seq):
    n = DIL_HALF_WINDOW
    qb = min(QB_DIL, seq)
    kw = min(qb + 2 * n, seq)
    n_heads = DIL_GROUP_WIDTH // HEAD_DIM
    lane = lax.broadcasted_iota(I32, (1, DIL_GROUP_WIDTH), 1)
    head_masks = [(lane >= h * HEAD_DIM) & (lane < (h + 1) * HEAD_DIM) for h in range(n_heads)]

    def rows(ref, start, count, r):
        if dil == 1:
            return ref[pl.ds(start, count), :]
        return ref[pl.ds(start * dil + r, count, stride=dil), :]

    def load(ref, start, count, r):
        lo, hi = _unpack_halves(rows(ref, start, count, r))
        return jnp.concatenate([lo, hi], axis=1).astype(BF16)

    def block(i, r):
        a = pl.multiple_of(i * qb, qb)
        ws = pl.multiple_of(jnp.clip(a - n, 0, seq - kw), n)
        q, k, v = load(q_ref, a, qb, r), load(k_ref, ws, kw, r), load(v_ref, ws, kw, r)
        qs = jnp.concatenate([jnp.where(hm, q, jnp.zeros_like(q)) for hm in head_masks], axis=0)
        s = lax.dot_general(qs, k, (((1,), (1,)), ((), ())), preferred_element_type=F32)
        qpos = a + lax.broadcasted_iota(I32, (qb, 1), 0)
        kpos = ws + lax.broadcasted_iota(I32, (1, kw), 1)
        valid = jnp.abs(kpos - qpos) <= n
        s = jnp.where(jnp.concatenate([valid] * n_heads, axis=0), s, NEG_INF)
        m = jnp.max(s, axis=1, keepdims=True)
        p = jnp.exp(s - m)
        l = jnp.sum(p, axis=1, keepdims=True)
        pv = jnp.dot(p.astype(BF16), v, preferred_element_type=F32) * (1.0 / l)
        lse = m + jnp.log(l)
        o_acc = jnp.zeros((qb, DIL_GROUP_WIDTH), F32)
        lse_acc = jnp.zeros((qb, DIL_GROUP_WIDTH), F32)
        for h, hm in enumerate(head_masks):
            o_acc = jnp.where(hm, pv[h * qb:(h + 1) * qb], o_acc)
            lse_acc = jnp.where(hm, lse[h * qb:(h + 1) * qb], lse_acc)
        packed = _pack_halves(o_acc)
        if dil == 1:
            dst = pl.ds(a, qb)
        else:
            dst = pl.ds(a * dil + r, qb, stride=dil)
        o_ref[dst, :] = packed
        lse_lo_ref[dst, :] = lse_acc[:, :DIL_PACKED]
        lse_hi_ref[dst, :] = lse_acc[:, DIL_PACKED:]

    for r in range(dil):
        def body(i, carry, r=r):
            block(i, r)
            return carry
        lax.fori_loop(0, seq // qb, body, 0, unroll=min(DIL_UNROLL, seq // qb))


def _dilated(qa, ka, va, g):
    _, B, S, W = qa.shape
    dil = DIL_CONFIGS[g][1]
    in_spec = pl.BlockSpec((None, None, S, W), lambda b: (g, b, 0, 0))
    out_spec = pl.BlockSpec((None, S, W), lambda b: (b, 0, 0))
    return pl.pallas_call(
        functools.partial(_dil_kernel, dil=dil, seq=S // dil),
        grid=(B,),
        in_specs=[in_spec, in_spec, in_spec],
        out_specs=[out_spec, out_spec, out_spec],
        out_shape=[jax.ShapeDtypeStruct((B, S, W), U32), jax.ShapeDtypeStruct((B, S, W), F32),
                   jax.ShapeDtypeStruct((B, S, W), F32)],
        compiler_params=_params(("arbitrary",)),
        name=f"dilated{g}",
    )(qa, ka, va)


def _gqa_kernel(q_ref, kt_ref, v_ref, o_ref, zero_ref, s0_ref, s1_ref, p0_ref, p1_ref, zbuf_ref, zsem,
                *, zero_chunks):
    step = pl.program_id(0) * pl.num_programs(1) + pl.program_id(1)
    last = pl.num_programs(0) * pl.num_programs(1) - 1
    zr = zbuf_ref.shape[0]

    def zero_copy(s, j):
        return pltpu.make_async_copy(zbuf_ref, zero_ref.at[pl.ds((s * zero_chunks + j) * zr, zr), :], zsem)

    @pl.when(step == 0)
    def _():
        zbuf_ref[...] = jnp.zeros_like(zbuf_ref)

    @pl.when(step > 0)
    def _():
        for j in range(zero_chunks):
            zero_copy(step - 1, j).wait()

    for j in range(zero_chunks):
        zero_copy(step, j).start()

    tq = TQ_GQA
    s_refs, p_refs = (s0_ref, s1_ref), (p0_ref, p1_ref)
    lane = lax.broadcasted_iota(I32, (1, GQA_KV_WIDTH), 1)
    head_masks = [(lane >= g * HEAD_DIM) & (lane < (g + 1) * HEAD_DIM) for g in range(GQA_KV_HEADS)]
    nrep = GQA_ITEM_REPS
    items = [(sub, c0, g) for sub in range(q_ref.shape[0] // tq) for c0 in range(0, GQA_REP, nrep)
             for g in range(GQA_KV_HEADS)]

    def scores(i):
        sub, c0, g = items[i]
        rows = slice(sub * tq, (sub + 1) * tq)
        qg = jnp.concatenate(
            [jnp.where(head_masks[g], q_ref[rows, c * 256:(c + 1) * 256], jnp.zeros((tq, 256), BF16))
             for c in range(c0, c0 + nrep)], axis=0)
        s_refs[i % 2][...] = jnp.dot(qg, kt_ref[...], preferred_element_type=F32)

    scores(0)
    outs = None
    for i, (sub, c0, g) in enumerate(items):
        if i + 1 < len(items):
            scores(i + 1)
        if g == 0:
            outs = [jnp.zeros((tq, GQA_KV_WIDTH), F32) for _ in range(nrep)]
        s = s_refs[i % 2][...]
        m = jnp.max(s, axis=1, keepdims=True)
        e = jnp.exp(s - m)
        l = jnp.sum(e, axis=1, keepdims=True)
        p_refs[i % 2][...] = e.astype(BF16)
        pv = jnp.dot(p_refs[i % 2][...], v_ref[...], preferred_element_type=F32) * (1.0 / l)
        for j in range(nrep):
            outs[j] = jnp.where(head_masks[g], pv[j * tq:(j + 1) * tq], outs[j])
        if g == GQA_KV_HEADS - 1:
            rows = slice(sub * tq, (sub + 1) * tq)
            for j in range(nrep):
                o_ref[rows, (c0 + j) * 256:(c0 + j + 1) * 256] = outs[j].astype(BF16)

    @pl.when(step == last)
    def _():
        for j in range(zero_chunks):
            zero_copy(step, j).wait()


def _gqa(qb, kbt, vb, rows_padded):
    B, S, _ = qb.shape
    tq = TQ_GQA * GQA_SUBTILES
    item_rows = TQ_GQA * GQA_ITEM_REPS
    steps = B * (S // tq)
    zero_rows = rows_padded * ROW_SUBLANES
    assert zero_rows % (steps * ZERO_CHUNK_ROWS) == 0
    zero_chunks = zero_rows // (steps * ZERO_CHUNK_ROWS)
    return pl.pallas_call(
        functools.partial(_gqa_kernel, zero_chunks=zero_chunks),
        grid=(B, S // tq),
        in_specs=[pl.BlockSpec((None, tq, GQA_Q_WIDTH), lambda b, i: (b, i, 0)),
                  pl.BlockSpec((None, GQA_KV_WIDTH, S), lambda b, i: (b, 0, 0)),
                  pl.BlockSpec((None, S, GQA_KV_WIDTH), lambda b, i: (b, 0, 0))],
        out_specs=[pl.BlockSpec((None, tq, GQA_Q_WIDTH), lambda b, i: (b, i, 0)),
                   pl.BlockSpec(memory_space=pl.ANY)],
        out_shape=[jax.ShapeDtypeStruct((B, S, GQA_Q_WIDTH), BF16),
                   jax.ShapeDtypeStruct((zero_rows, LANES), U32)],
        scratch_shapes=[pltpu.VMEM((item_rows, S), F32), pltpu.VMEM((item_rows, S), F32),
                        pltpu.VMEM((item_rows, S), BF16), pltpu.VMEM((item_rows, S), BF16),
                        pltpu.VMEM((ZERO_CHUNK_ROWS, LANES), U32), pltpu.SemaphoreType.DMA(())],
        compiler_params=_params(("arbitrary", "arbitrary")),
        name="gqa",
    )(qb, kbt, vb)


def _pack_halves(y):
    half = y.shape[1] // 2
    lo = lax.bitcast_convert_type(y[:, :half].astype(BF16).astype(F32), U32)
    hi = lax.bitcast_convert_type(y[:, half:].astype(BF16).astype(F32), U32)
    return (lo >> 16) | hi


def _unpack_halves(u):
    lo = lax.bitcast_convert_type(u << 16, F32)
    hi = lax.bitcast_convert_type(u & jnp.uint32(0xFFFF0000), F32)
    return lo, hi


ROW_SUBLANES = HALF // LANES


def _store_rows(ref, packed):
    m = packed.shape[0]
    for c in range(ROW_SUBLANES):
        ref[pl.ds(c, m, stride=ROW_SUBLANES), :] = packed[:, c * LANES:(c + 1) * LANES]


def _load_rows(ref, m):
    return jnp.concatenate([ref[pl.ds(c, m, stride=ROW_SUBLANES), :] for c in range(ROW_SUBLANES)], axis=1)


def _mix_kernel(x_ref, *refs):
    dil_refs, refs = refs[:3 * N_DIL_GROUPS], refs[3 * N_DIL_GROUPS:]
    (yb_ref, ga_ref, gb_ref, mod_ref, wod_ref, wog_ref, wout_ref, g2_ref, rw_ref, wsg_ref, wsu_ref, wsd_ref,
     xs_ref, h2_ref, logit_ref) = refs

    outs = [_unpack_halves(dil_refs[3 * g][0]) for g in range(N_DIL_GROUPS)]
    halves = []
    for part in range(2):
        lses = [dil_refs[3 * g + 1 + part][0] for g in range(N_DIL_GROUPS)]
        mx = functools.reduce(jnp.maximum, lses)
        es = [jnp.exp(l - mx) for l in lses]
        num = sum(e * outs[g][part] for g, e in enumerate(es))
        halves.append(num / sum(es))
    ya = jnp.concatenate(halves, axis=1)
    y_a = jnp.dot(ya.astype(BF16), wod_ref[...], preferred_element_type=F32)
    y_b = jnp.dot(yb_ref[0], wog_ref[...], preferred_element_type=F32)
    mix = ga_ref[0].astype(F32) * y_a + gb_ref[0].astype(F32) * y_b
    mixed = jnp.dot(mix.astype(BF16), wout_ref[...], preferred_element_type=F32)
    x1 = x_ref[0] + mod_ref[0, 2:3, :] * mixed

    ms = jnp.mean(x1 * x1, axis=-1, keepdims=True)
    h2 = x1 * lax.rsqrt(ms + NORM_EPS) * g2_ref[...]
    h2 = h2 * (1.0 + mod_ref[0, 4:5, :]) + mod_ref[0, 3:4, :]
    h2b = h2.astype(BF16)
    _store_rows(h2_ref, _pack_halves(h2))
    logit_ref[0] = jnp.dot(h2b, rw_ref[...], preferred_element_type=F32)

    gt = jnp.dot(h2b, wsg_ref[...], preferred_element_type=F32)
    up = jnp.dot(h2b, wsu_ref[...], preferred_element_type=F32)
    act = (gt * jax.nn.sigmoid(gt) * up).astype(BF16)
    shared = jnp.dot(act, wsd_ref[...], preferred_element_type=F32)
    xs_ref[0] = x1 + mod_ref[0, 5:6, :] * shared


def _mix(x, dil_outs, yb, ga, gb, mod, weights):
    B, S, _ = x.shape
    tm = TM_MIX
    tok = lambda w: pl.BlockSpec((1, tm, w), lambda b, i: (b, i, 0))
    dil_flat = [a for group in dil_outs for a in group]
    in_specs = ([tok(D_MODEL)] + [tok(DIL_PACKED)] * len(dil_flat) + [tok(GQA_Q_WIDTH), tok(D_MODEL), tok(D_MODEL),
                pl.BlockSpec((1, 6, D_MODEL), lambda b, i: (b, 0, 0))] + [_const_spec(w.shape) for w in weights])
    return pl.pallas_call(
        _mix_kernel,
        grid=(B, S // tm),
        in_specs=in_specs,
        out_specs=[tok(D_MODEL),
                   pl.BlockSpec((tm * ROW_SUBLANES, LANES), lambda b, i: (b * (S // tm) + i, 0)),
                   tok(N_EXPERTS)],
        out_shape=[jax.ShapeDtypeStruct((B, S, D_MODEL), F32),
                   jax.ShapeDtypeStruct((B * S * ROW_SUBLANES, LANES), U32),
                   jax.ShapeDtypeStruct((B, S, N_EXPERTS), F32)],
        compiler_params=_params(("arbitrary", "arbitrary")),
        name="mix",
    )(x, *dil_flat, yb, ga, gb, mod, *weights)


def _first_argmax(v, idx, big):
    m = jnp.max(v, axis=0, keepdims=True)
    i = jnp.min(jnp.where(v == m, idx, big), axis=0, keepdims=True)
    return m, i


def _route_kernel(logit_ref, bias_ref, tri_ref, idx_ref, gate_ref, gate_tm_ref, rank_ref, cnt_ref, base_ref):
    step = pl.program_id(0)
    tr = logit_ref.shape[0]

    @pl.when(step == 0)
    def _():
        base_ref[...] = jnp.zeros_like(base_ref)

    scores = jax.nn.sigmoid(logit_ref[...].T)
    biased = scores + bias_ref[...]
    row = lax.broadcasted_iota(I32, (N_EXPERTS, tr), 0)

    gscore = []
    group_row = lax.broadcasted_iota(I32, (GROUP_SIZE, tr), 0)
    for g in range(N_EXPERT_GROUPS):
        v, ri = biased[g * GROUP_SIZE:(g + 1) * GROUP_SIZE], group_row + g * GROUP_SIZE
        m1, i1 = _first_argmax(v, ri, N_EXPERTS)
        m2 = jnp.max(jnp.where(ri == i1, NEG_INF, v), axis=0, keepdims=True)
        gscore.append(m1 + m2)
    cur = jnp.concatenate(gscore, axis=0)
    gi = lax.broadcasted_iota(I32, (N_EXPERT_GROUPS, tr), 0)
    sel = jnp.zeros((N_EXPERT_GROUPS, tr), F32)
    for _ in range(TOPK_GROUPS):
        _, i = _first_argmax(cur, gi, N_EXPERT_GROUPS)
        hit = gi == i
        sel = jnp.where(hit, 1.0, sel)
        cur = jnp.where(hit, NEG_INF, cur)

    cur = jnp.concatenate(
        [jnp.where(sel[g:g + 1] > 0, biased[g * GROUP_SIZE:(g + 1) * GROUP_SIZE], NEG_INF)
         for g in range(N_EXPERT_GROUPS)], axis=0)
    idxs, gates, hits = [], [], []
    assigned = jnp.zeros((N_EXPERTS, tr), F32)
    for _ in range(TOP_K):
        _, i = _first_argmax(cur, row, N_EXPERTS)
        hit = row == i
        gates.append(jnp.sum(jnp.where(hit, scores, 0.0), axis=0, keepdims=True))
        cur = jnp.where(hit, NEG_INF, cur)
        assigned = jnp.where(hit, 1.0, assigned)
        idxs.append(i)
        hits.append(hit)
    gate = jnp.concatenate(gates, axis=0)
    gate = gate / jnp.sum(gate, axis=0, keepdims=True) * ROUTED_SCALE

    before = jnp.dot(assigned.astype(BF16), tri_ref[...], preferred_element_type=F32) + base_ref[:, 0:1]
    ranks = [jnp.sum(jnp.where(h, before, 0.0), axis=0, keepdims=True) for h in hits]

    idx_ref[...] = jnp.concatenate(idxs, axis=0)
    gate_ref[...] = gate
    rank_ref[...] = jnp.concatenate(ranks, axis=0).astype(I32)
    gate_tm_ref[...] = jnp.concatenate([gate, jnp.zeros((LANES - TOP_K, tr), F32)], axis=0).T
    base_ref[...] = base_ref[...] + jnp.sum(assigned, axis=1, keepdims=True)
    cnt_ref[...] = base_ref[...]


def _route(logits, bias):
    T = logits.shape[0]
    tr = TR_ROUTE
    tri = (lax.broadcasted_iota(I32, (tr, tr), 0) < lax.broadcasted_iota(I32, (tr, tr), 1)).astype(BF16)
    slot = pl.BlockSpec((TOP_K, tr), lambda i: (0, i))
    return pl.pallas_call(
        _route_kernel,
        grid=(T // tr,),
        in_specs=[pl.BlockSpec((tr, N_EXPERTS), lambda i: (i, 0)), _const_spec((N_EXPERTS, 1)), _const_spec((tr, tr))],
        out_specs=[slot, slot, pl.BlockSpec((tr, LANES), lambda i: (i, 0)), slot, _const_spec((N_EXPERTS, LANES))],
        out_shape=[jax.ShapeDtypeStruct((TOP_K, T), I32), jax.ShapeDtypeStruct((TOP_K, T), F32),
                   jax.ShapeDtypeStruct((T, LANES), F32), jax.ShapeDtypeStruct((TOP_K, T), I32),
                   jax.ShapeDtypeStruct((N_EXPERTS, LANES), F32)],
        scratch_shapes=[pltpu.VMEM((N_EXPERTS, LANES), F32)],
        compiler_params=_params(("arbitrary",)),
        name="route",
    )(logits, bias.reshape(N_EXPERTS, 1), tri)


def _plan_kernel(cnt_ref, idx_ref, rank_ref, ltri_ref, dest_ref, bexp_ref, nblk_ref, *, nb_max):
    tr = idx_ref.shape[1]
    blocks = jnp.floor((cnt_ref[...] + (BM_EXPERT - 1)) * (1.0 / BM_EXPERT))
    start = jnp.dot(ltri_ref[...], blocks.astype(BF16), preferred_element_type=F32)
    end = start + blocks
    blk = lax.broadcasted_iota(I32, (1, nb_max), 1).astype(F32)
    bexp = jnp.sum(jnp.where(end[:, 0:1] <= blk, 1.0, 0.0), axis=0, keepdims=True)
    bexp_ref[...] = jnp.minimum(bexp, N_EXPERTS - 1.0).astype(I32)
    nblk_ref[...] = jnp.sum(blocks, axis=0, keepdims=True).astype(I32)

    row = lax.broadcasted_iota(I32, (N_EXPERTS, tr), 0)
    start_rows = start[:, 0:1] * float(BM_EXPERT)
    dests = []
    for k in range(TOP_K):
        hit = row == idx_ref[k:k + 1, :]
        dests.append(jnp.sum(jnp.where(hit, start_rows, 0.0), axis=0, keepdims=True))
    dest = jnp.concatenate(dests, axis=0) + rank_ref[...].astype(F32)
    dest_ref[...] = jnp.concatenate([dest, jnp.zeros((LANES - TOP_K, tr), F32)], axis=0).T.astype(I32)


def _plan(counts, idx, rank, nb_max):
    T = idx.shape[1]
    tr = TR_ROUTE
    ltri = (lax.broadcasted_iota(I32, (N_EXPERTS, N_EXPERTS), 1)
            < lax.broadcasted_iota(I32, (N_EXPERTS, N_EXPERTS), 0)).astype(BF16)
    slot = pl.BlockSpec((TOP_K, tr), lambda i: (0, i))
    return pl.pallas_call(
        functools.partial(_plan_kernel, nb_max=nb_max),
        grid=(T // tr,),
        in_specs=[_const_spec((N_EXPERTS, LANES)), slot, slot, _const_spec((N_EXPERTS, N_EXPERTS))],
        out_specs=[pl.BlockSpec((tr, LANES), lambda i: (i, 0)), _const_spec((1, nb_max)), _const_spec((1, LANES))],
        out_shape=[jax.ShapeDtypeStruct((T, LANES), I32), jax.ShapeDtypeStruct((1, nb_max), I32),
                   jax.ShapeDtypeStruct((1, LANES), I32)],
        compiler_params=_params(("arbitrary",)),
        name="plan",
    )(counts, idx, rank, ltri)


def _dispatch_kernel(dest_ref, h_ref, xs_in_ref, xs_ref, sem):
    del xs_in_ref
    tm = h_ref.shape[0]

    def issue(t, carry):
        for k in range(TOP_K):
            pltpu.make_async_copy(h_ref.at[t], xs_ref.at[dest_ref[t * TOP_K + k]], sem).start(priority=k % 2)
        return carry

    lax.fori_loop(0, tm, issue, 0)
    for _ in range(TOP_K):
        pltpu.make_async_copy(h_ref, xs_ref.at[pl.ds(0, tm)], sem).wait()


def _dispatch(dest_flat, h_rows, xs0):
    T = h_rows.shape[0]
    rows_padded = xs0.shape[0]
    tm = TM_DISPATCH
    return pl.pallas_call(
        _dispatch_kernel,
        grid=(T // tm,),
        in_specs=[pl.BlockSpec((tm * TOP_K,), lambda i: (i,), memory_space=pltpu.SMEM),
                  pl.BlockSpec((tm, ROW_SUBLANES, LANES), lambda i: (i, 0, 0)),
                  pl.BlockSpec(memory_space=pl.ANY)],
        out_specs=pl.BlockSpec(memory_space=pl.ANY),
        out_shape=jax.ShapeDtypeStruct((rows_padded, ROW_SUBLANES, LANES), U32),
        scratch_shapes=[pltpu.SemaphoreType.DMA(())],
        input_output_aliases={2: 0},
        compiler_params=_params(("arbitrary",)),
        name="dispatch",
    )(dest_flat, h_rows, xs0)


def _expert_kernel(bexp_ref, nblk_ref, xs_ref, wg_ref, wu_ref, wd_ref, ys_ref, wgb_ref, wub_ref, wdb_ref,
                   xbuf_ref, xsem, obuf_ref, osem):
    p = pl.program_id(0)
    nb = nblk_ref[0]
    prev = bexp_ref[jnp.maximum(p - 1, 0)]
    active = p < nb
    block_rows = BM_EXPERT * ROW_SUBLANES

    def block(ref, q):
        return ref.at[pl.ds(pl.multiple_of(q * block_rows, block_rows), block_rows), :]

    def fetch(q):
        return pltpu.make_async_copy(block(xs_ref, q), xbuf_ref.at[q % XS_RING], xsem.at[q % XS_RING])

    def writeback(q):
        return pltpu.make_async_copy(obuf_ref.at[q % 2], block(ys_ref, q), osem.at[q % 2])

    @pl.when(p == 0)
    def _():
        fetch(0).start()

        @pl.when(nb > 1)
        def _():
            fetch(1).start()

    @pl.when(p + 2 < nb)
    def _():
        fetch(p + 2).start()

    @pl.when(active & ((p == 0) | (bexp_ref[p] != prev)))
    def _():
        wgb_ref[...] = wg_ref[...].astype(BF16)
        wub_ref[...] = wu_ref[...].astype(BF16)
        wdb_ref[...] = wd_ref[...].astype(BF16)

    @pl.when(active)
    def _():
        fetch(p).wait()
        lo, hi = _unpack_halves(_load_rows(xbuf_ref.at[p % XS_RING], BM_EXPERT))
        lo, hi = lo.astype(BF16), hi.astype(BF16)

        def proj(w_ref):
            return (jnp.dot(lo, w_ref[:HALF, :], preferred_element_type=F32)
                    + jnp.dot(hi, w_ref[HALF:, :], preferred_element_type=F32))

        gt, up = proj(wgb_ref), proj(wub_ref)
        act = (gt * jax.nn.sigmoid(gt) * up).astype(BF16)
        _store_rows(obuf_ref.at[p % 2], _pack_halves(jnp.dot(act, wdb_ref[...], preferred_element_type=F32)))
        writeback(p).start()

        @pl.when(p > 0)
        def _():
            writeback(p - 1).wait()

        @pl.when(p == nb - 1)
        def _():
            writeback(p).wait()


def _experts(bexp, nblk, xs, w_gate, w_up, w_down, nb_max):
    bm = BM_EXPERT
    wspec = lambda shape: pl.BlockSpec((None,) + shape, lambda p, be, nb: (be[p], 0, 0))
    block_shape = (bm * ROW_SUBLANES, LANES)
    grid_spec = pltpu.PrefetchScalarGridSpec(
        num_scalar_prefetch=2,
        grid=(nb_max,),
        in_specs=[pl.BlockSpec(memory_space=pl.ANY), wspec((D_MODEL, EXPERT_FF)),
                  wspec((D_MODEL, EXPERT_FF)), wspec((EXPERT_FF, D_MODEL))],
        out_specs=pl.BlockSpec(memory_space=pl.ANY),
        scratch_shapes=[pltpu.VMEM((D_MODEL, EXPERT_FF), BF16), pltpu.VMEM((D_MODEL, EXPERT_FF), BF16),
                        pltpu.VMEM((EXPERT_FF, D_MODEL), BF16),
                        pltpu.VMEM((XS_RING,) + block_shape, U32), pltpu.SemaphoreType.DMA((XS_RING,)),
                        pltpu.VMEM((2,) + block_shape, U32), pltpu.SemaphoreType.DMA((2,))],
    )
    return pl.pallas_call(
        _expert_kernel,
        grid_spec=grid_spec,
        out_shape=jax.ShapeDtypeStruct(xs.shape, U32),
        input_output_aliases={2: 0},
        compiler_params=_params(("arbitrary",)),
        name="experts",
    )(bexp, nblk, xs, w_gate, w_up, w_down)


def _combine_kernel(dest_ref, dnext_ref, xs_ref, gate_ref, mod_ref, fg_ref, ys_ref, ys2d_ref, o_ref, buf_ref, sem):
    i = pl.program_id(0)
    n = pl.num_programs(0)
    tm = o_ref.shape[0]
    slot = i % 2

    def row_copy(d_ref, s, t, k):
        return pltpu.make_async_copy(ys_ref.at[d_ref[t * TOP_K + k]],
                                     buf_ref.at[s, k, pl.ds(t * ROW_SUBLANES, ROW_SUBLANES), :], sem.at[s])

    @pl.when(i == 0)
    def _():
        def body(t, carry):
            for k in range(TOP_K):
                row_copy(dest_ref, 0, t, k).start(priority=k % 2)
            return carry
        lax.fori_loop(0, tm, body, 0)

    def finish(prefetch):
        for k in range(TOP_K):
            pltpu.make_async_copy(ys2d_ref.at[pl.ds(0, tm * ROW_SUBLANES), :], buf_ref.at[slot, k],
                                  sem.at[slot]).wait()
        acc_lo = jnp.zeros((tm, HALF), F32)
        acc_hi = jnp.zeros((tm, HALF), F32)
        batch = tm // TOP_K
        for k in range(TOP_K):
            if prefetch:
                for t in range(k * batch, (k + 1) * batch):
                    for kk in range(TOP_K):
                        row_copy(dnext_ref, 1 - slot, t, kk).start(priority=kk % 2)
            lo, hi = _unpack_halves(_load_rows(buf_ref.at[slot, k], tm))
            g = gate_ref[:, k:k + 1]
            acc_lo = acc_lo + g * lo
            acc_hi = acc_hi + g * hi
        routed = jnp.concatenate([acc_lo, acc_hi], axis=1)
        x2 = xs_ref[...] + mod_ref[0, 5:6, :] * routed
        ms = jnp.mean(x2 * x2, axis=-1, keepdims=True)
        o_ref[...] = x2 * lax.rsqrt(ms + NORM_EPS) * fg_ref[...]

    @pl.when(i + 1 < n)
    def _():
        finish(True)

    @pl.when(i + 1 >= n)
    def _():
        finish(False)


def _combine(dest_flat, xs_mid, gate_tm, mod, final_g, ys2d, seq):
    T = xs_mid.shape[0]
    tm = TM_COMBINE
    n = T // tm
    per_seq = seq // tm
    ys_rows = ys2d.reshape(-1, ROW_SUBLANES, LANES)
    return pl.pallas_call(
        _combine_kernel,
        grid=(n,),
        in_specs=[pl.BlockSpec((tm * TOP_K,), lambda i: (i,), memory_space=pltpu.SMEM),
                  pl.BlockSpec((tm * TOP_K,), lambda i: (jnp.minimum(i + 1, n - 1),), memory_space=pltpu.SMEM),
                  pl.BlockSpec((tm, D_MODEL), lambda i: (i, 0)),
                  pl.BlockSpec((tm, LANES), lambda i: (i, 0)),
                  pl.BlockSpec((1, 6, D_MODEL), lambda i: (i // per_seq, 0, 0)),
                  _const_spec((1, D_MODEL)),
                  pl.BlockSpec(memory_space=pl.ANY),
                  pl.BlockSpec(memory_space=pl.ANY)],
        out_specs=pl.BlockSpec((tm, D_MODEL), lambda i: (i, 0)),
        out_shape=jax.ShapeDtypeStruct((T, D_MODEL), F32),
        scratch_shapes=[pltpu.VMEM((2, TOP_K, tm * ROW_SUBLANES, LANES), U32), pltpu.SemaphoreType.DMA((2,))],
        compiler_params=_params(("arbitrary",)),
        name="combine",
    )(dest_flat, dest_flat, xs_mid, gate_tm, mod, final_g, ys_rows, ys2d)


def _rope_tables(seq):
    def cos_sin(pos, dim):
        inv = ROPE_THETA ** (-jnp.arange(0, dim, 2, dtype=F32) / dim)
        ang = pos.astype(F32)[:, None] * inv[None, :]
        ang = jnp.concatenate([ang, ang], axis=-1)
        sign = jnp.where(jnp.arange(dim) < dim // 2, -1.0, 1.0).astype(F32)
        return jnp.cos(ang), jnp.sin(ang) * sign

    t = jnp.arange(seq, dtype=I32)
    cos1, sin1 = cos_sin(t, HEAD_DIM)
    cos_r, sin_r = cos_sin(t // GRID_W, HEAD_DIM // 2)
    cos_c, sin_c = cos_sin(t % GRID_W, HEAD_DIM // 2)
    cosb = jnp.concatenate([cos_r, cos_c], axis=-1)
    sinb = jnp.concatenate([sin_r, sin_c], axis=-1)
    rep = LANES // HEAD_DIM
    return tuple(jnp.tile(a, (1, rep)) for a in (cos1, sin1, cosb, sinb))


def kernel(x, c, w_ada, b_ada, norm1_g, w_in, b_gate, qn_g, kn_g, w_o_dil, w_o_gqa, w_out, norm2_g,
           router_w, router_bias, w_exp_gate, w_exp_up, w_exp_down, w_sh_gate, w_sh_up, w_sh_down, final_g):
    B, S, D = x.shape
    T = B * S
    assert D == D_MODEL and S % (TM_PROJ * 1) == 0 and S // 16 >= QB_DIL
    row = lambda v: v.reshape(1, -1)

    mod = _adaln(c, w_ada, b_ada).reshape(B, 6, D)

    wb = w_in.astype(BF16)
    offs = [0, DIL_WIDTH, 2 * DIL_WIDTH, 3 * DIL_WIDTH, 3 * DIL_WIDTH + GQA_Q_WIDTH,
            3 * DIL_WIDTH + GQA_Q_WIDTH + GQA_KV_WIDTH, 3 * DIL_WIDTH + GQA_Q_WIDTH + 2 * GQA_KV_WIDTH]
    w_qa, w_ka, w_va = (wb[:, offs[i]:offs[i + 1]] for i in range(3))
    w_qb = wb[:, offs[3]:offs[4]].reshape(D, GQA_KV_HEADS, GQA_REP, HEAD_DIM).transpose(0, 2, 1, 3).reshape(D, GQA_Q_WIDTH)
    w_kb, w_vb = wb[:, offs[4]:offs[5]], wb[:, offs[5]:offs[6]]
    w_ga, w_gb = wb[:, offs[6]:offs[6] + D], wb[:, offs[6] + D:offs[6] + 2 * D]
    w_og = w_o_gqa.reshape(GQA_KV_HEADS, GQA_REP, HEAD_DIM, D).transpose(1, 0, 2, 3).reshape(GQA_Q_WIDTH, D).astype(BF16)

    lane_head = lax.broadcasted_iota(I32, (256, 256), 0) // HEAD_DIM
    bd = (lane_head == lane_head.T).astype(BF16)
    qn = jnp.tile(row(qn_g), (1, GQA_Q_WIDTH // HEAD_DIM))
    kn = jnp.tile(row(kn_g), (1, GQA_KV_WIDTH // HEAD_DIM))

    nb_max = -(-(T * TOP_K) // BM_EXPERT) + N_EXPERTS
    nb_max = -(-nb_max // LANES) * LANES
    rows_padded = nb_max * BM_EXPERT

    qa, ka, va, qb, kbt, vb, ga, gb = _in_proj(
        x, mod, row(norm1_g), (w_qa, w_ka, w_va, w_qb, w_kb, w_vb, w_ga, w_gb), row(b_gate), qn, kn,
        _rope_tables(S), bd)

    dil_outs = [_dilated(qa, ka, va, g) for g in range(N_DIL_GROUPS)]
    yb, xs0 = _gqa(qb, kbt, vb, rows_padded)

    xs_mid, h2p, logits = _mix(
        x, dil_outs, yb, ga, gb, mod,
        (w_o_dil.astype(BF16), w_og, w_out.astype(BF16), row(norm2_g), router_w.astype(BF16),
         w_sh_gate.astype(BF16), w_sh_up.astype(BF16), w_sh_down.astype(BF16)))

    idx, gate, gate_tm, rank, counts = _route(logits.reshape(T, N_EXPERTS), router_bias)
    del gate
    dest_tm, bexp, nblk = _plan(counts, idx, rank, nb_max)
    dest_flat = dest_tm[:, :TOP_K].reshape(T * TOP_K)

    xs = _dispatch(dest_flat, h2p.reshape(T, ROW_SUBLANES, LANES), xs0.reshape(rows_padded, ROW_SUBLANES, LANES))
    ys = _experts(bexp.reshape(nb_max), nblk.reshape(LANES)[:1], xs.reshape(rows_padded * ROW_SUBLANES, LANES),
                  w_exp_gate, w_exp_up, w_exp_down, nb_max)
    out = _combine(dest_flat, xs_mid.reshape(T, D), gate_tm, mod, row(final_g), ys, S)
    return out.reshape(B, S, D)
```

```python
import functools

import jax
import jax.numpy as jnp
from jax import lax
from jax.experimental import pallas as pl
from jax.experimental.pallas import tpu as pltpu

F32 = jnp.float32
BF16 = jnp.bfloat16
U32 = jnp.uint32
I32 = jnp.int32

D_MODEL = 1024
HEAD_DIM = 64
DIL_CONFIGS = ((128, 1), (512, 4), (2048, 16))
N_DIL_GROUPS = 3
DIL_GROUP_WIDTH = 256
DIL_WIDTH = N_DIL_GROUPS * DIL_GROUP_WIDTH
DIL_PACKED = DIL_GROUP_WIDTH // 2
DIL_HALF_WINDOW = 64
GQA_Q_WIDTH = 1024
GQA_KV_WIDTH = 256
GQA_KV_HEADS = 4
GQA_REP = 4
GRID_W = 64
ROPE_THETA = 10000.0
N_EXPERTS = 256
TOP_K = 8
N_EXPERT_GROUPS = 8
GROUP_SIZE = N_EXPERTS // N_EXPERT_GROUPS
TOPK_GROUPS = 4
EXPERT_FF = 256
ROUTED_SCALE = 2.5
NORM_EPS = 1e-6
HALF = D_MODEL // 2

LANES = 128
VMEM_LIMIT = 56 * 1024 * 1024

TM_PROJ = 256
TM_MIX = 512
XS_RING = 3
TQ_GQA = 256
GQA_SUBTILES = 2
GQA_ITEM_REPS = 2
QB_DIL = 128
DIL_UNROLL = 2
TR_ROUTE = 512
TM_DISPATCH = 1024
TM_COMBINE = 128
BM_EXPERT = 512

NEG_INF = float("-inf")


def _params(sem):
    return pltpu.CompilerParams(dimension_semantics=sem, vmem_limit_bytes=VMEM_LIMIT)


def _const_spec(shape, single_buffer=False):
    n = len(shape)
    if single_buffer:
        return pl.BlockSpec(shape, lambda *_: (0,) * n, pipeline_mode=pl.Buffered(1))
    return pl.BlockSpec(shape, lambda *_: (0,) * n)


def _adaln_kernel(c_ref, w_ref, b_ref, o_ref):
    c = c_ref[...]
    a = c * jax.nn.sigmoid(c)
    o_ref[...] = jnp.dot(a, w_ref[...], preferred_element_type=F32,
                         precision=lax.Precision.HIGHEST) + b_ref[...]


def _adaln(c, w_ada, b_ada):
    B = c.shape[0]
    return pl.pallas_call(
        _adaln_kernel,
        grid=(6,),
        in_specs=[pl.BlockSpec((B, D_MODEL), lambda j: (0, 0)),
                  pl.BlockSpec((D_MODEL, D_MODEL), lambda j: (0, j)),
                  pl.BlockSpec((1, D_MODEL), lambda j: (0, j))],
        out_specs=pl.BlockSpec((B, D_MODEL), lambda j: (0, j)),
        out_shape=jax.ShapeDtypeStruct((B, 6 * D_MODEL), F32),
        compiler_params=_params(("arbitrary",)),
        name="adaln",
    )(c, w_ada, b_ada.reshape(1, 6 * D_MODEL))


def _tile_lanes(t, width):
    return jnp.concatenate([t] * (width // LANES), axis=1)


def _rope(v, cos, sin_signed, half):
    w = v.shape[-1]
    lane = lax.broadcasted_iota(I32, (1, w), 1)
    first = (lane % (2 * half)) < half
    up = pltpu.roll(v, w - half, axis=1)
    dn = pltpu.roll(v, half, axis=1)
    return v * _tile_lanes(cos, w) + jnp.where(first, up, dn) * _tile_lanes(sin_signed, w)


def _head_rms(v, gain, bd_ref):
    w = v.shape[-1]
    parts = []
    for c0 in range(0, w, 256):
        vc = v[:, c0:c0 + 256]
        ms = jnp.dot((vc * vc).astype(BF16), bd_ref[...], preferred_element_type=F32) * (1.0 / HEAD_DIM)
        parts.append(vc * lax.rsqrt(ms + NORM_EPS))
    return jnp.concatenate(parts, axis=1) * gain


def _in_proj_kernel(x_ref, mod_ref, g1_ref, wqa_ref, wka_ref, wva_ref, wqb_ref, wkb_ref, wvb_ref,
                    wga_ref, wgb_ref, bg_ref, qn_ref, kn_ref, cos1_ref, sin1_ref, cosb_ref, sinb_ref,
                    bd_ref, qa_ref, ka_ref, va_ref, qb_ref, kbt_ref, vb_ref, ga_ref, gb_ref, zero_ref):
    zero_ref[...] = jnp.zeros_like(zero_ref)
    x = x_ref[0]
    ms = jnp.mean(x * x, axis=-1, keepdims=True)
    h = x * lax.rsqrt(ms + NORM_EPS) * g1_ref[...]
    h = h * (1.0 + mod_ref[0, 1:2, :]) + mod_ref[0, 0:1, :]
    hb = h.astype(BF16)

    def proj(w_ref):
        return jnp.dot(hb, w_ref[...], preferred_element_type=F32)

    cos1, sin1 = cos1_ref[...], sin1_ref[...]
    cosb, sinb = cosb_ref[...], sinb_ref[...]

    qa = _rope(proj(wqa_ref), cos1, sin1, HEAD_DIM // 2) * (HEAD_DIM ** -0.5)
    ka = _rope(proj(wka_ref), cos1, sin1, HEAD_DIM // 2)
    va = proj(wva_ref)
    for g in range(N_DIL_GROUPS):
        sl = slice(g * DIL_GROUP_WIDTH, (g + 1) * DIL_GROUP_WIDTH)
        qa_ref[g, 0] = _pack_halves(qa[:, sl])
        ka_ref[g, 0] = _pack_halves(ka[:, sl])
        va_ref[g, 0] = _pack_halves(va[:, sl])

    qb = _head_rms(proj(wqb_ref), qn_ref[...], bd_ref)
    qb_ref[0] = (_rope(qb, cosb, sinb, HEAD_DIM // 4) * (HEAD_DIM ** -0.5)).astype(BF16)
    kb = _rope(_head_rms(proj(wkb_ref), kn_ref[...], bd_ref), cosb, sinb, HEAD_DIM // 4)
    kbt_ref[0] = kb.T.astype(BF16)
    vb_ref[0] = proj(wvb_ref).astype(BF16)

    ga_ref[0] = jax.nn.sigmoid(proj(wga_ref) + bg_ref[:, :D_MODEL]).astype(BF16)
    gb_ref[0] = jax.nn.sigmoid(proj(wgb_ref) + bg_ref[:, D_MODEL:]).astype(BF16)


def _in_proj(x, mod, g1, weights, bgate, qn, kn, tables, bd, rows_padded):
    B, S, _ = x.shape
    tm = TM_PROJ
    steps = B * (S // tm)
    zero_rows = rows_padded * ROW_SUBLANES
    assert zero_rows % steps == 0
    zblock = zero_rows // steps
    tok = lambda w: pl.BlockSpec((1, tm, w), lambda b, i: (b, i, 0))
    tab = pl.BlockSpec((tm, LANES), lambda b, i: (i, 0))
    dil_out = pl.BlockSpec((N_DIL_GROUPS, 1, tm, DIL_PACKED), lambda b, i: (0, b, i, 0))
    dil_shape = jax.ShapeDtypeStruct((N_DIL_GROUPS, B, S, DIL_PACKED), U32)
    in_specs = ([tok(D_MODEL), pl.BlockSpec((1, 6, D_MODEL), lambda b, i: (b, 0, 0)), _const_spec((1, D_MODEL))]
                + [_const_spec(w.shape, single_buffer=True) for w in weights]
                + [_const_spec(bgate.shape), _const_spec(qn.shape), _const_spec(kn.shape), tab, tab, tab, tab,
                   _const_spec(bd.shape)])
    out_specs = [dil_out, dil_out, dil_out, tok(GQA_Q_WIDTH),
                 pl.BlockSpec((1, GQA_KV_WIDTH, tm), lambda b, i: (b, 0, i)), tok(GQA_KV_WIDTH),
                 tok(D_MODEL), tok(D_MODEL),
                 pl.BlockSpec((zblock, LANES), lambda b, i: (b * (S // tm) + i, 0))]
    out_shape = [dil_shape, dil_shape, dil_shape,
                 jax.ShapeDtypeStruct((B, S, GQA_Q_WIDTH), BF16),
                 jax.ShapeDtypeStruct((B, GQA_KV_WIDTH, S), BF16),
                 jax.ShapeDtypeStruct((B, S, GQA_KV_WIDTH), BF16),
                 jax.ShapeDtypeStruct((B, S, D_MODEL), BF16),
                 jax.ShapeDtypeStruct((B, S, D_MODEL), BF16),
                 jax.ShapeDtypeStruct((zero_rows, LANES), U32)]
    return pl.pallas_call(
        _in_proj_kernel,
        grid=(B, S // tm),
        in_specs=in_specs,
        out_specs=out_specs,
        out_shape=out_shape,
        compiler_params=_params(("arbitrary", "arbitrary")),
        name="in_proj",
    )(x, mod, g1, *weights, bgate, qn, kn, *tables, bd)


def _dil_kernel(q_ref, k_ref, v_ref, o_ref, lse_lo_ref, lse_hi_ref, *, dil, seq):
    n = DIL_HALF_WINDOW
    qb = min(QB_DIL, seq)
    kw = min(qb + 2 * n, seq)
    n_heads = DIL_GROUP_WIDTH // HEAD_DIM
    lane = lax.broadcasted_iota(I32, (1, DIL_GROUP_WIDTH), 1)
    head_masks = [(lane >= h * HEAD_DIM) & (lane < (h + 1) * HEAD_DIM) for h in range(n_heads)]

    def rows(ref, start, count, r):
        if dil == 1:
            return ref[pl.ds(start, count), :]
        return ref[pl.ds(start * dil + r, count, stride=dil), :]

    def load(ref, start, count, r):
        lo, hi = _unpack_halves(rows(ref, start, count, r))
        return jnp.concatenate([lo, hi], axis=1).astype(BF16)

    def block(i, r):
        a = pl.multiple_of(i * qb, qb)
        ws = pl.multiple_of(jnp.clip(a - n, 0, seq - kw), n)
        q, k, v = load(q_ref, a, qb, r), load(k_ref, ws, kw, r), load(v_ref, ws, kw, r)
        qs = jnp.concatenate([jnp.where(hm, q, jnp.zeros_like(q)) for hm in head_masks], axis=0)
        s = lax.dot_general(qs, k, (((1,), (1,)), ((), ())), preferred_element_type=F32)
        qpos = a + lax.broadcasted_iota(I32, (qb, 1), 0)
        kpos = ws + lax.broadcasted_iota(I32, (1, kw), 1)
        valid = jnp.abs(kpos - qpos) <= n
        s = jnp.where(jnp.concatenate([valid] * n_heads, axis=0), s, NEG_INF)
        m = jnp.max(s, axis=1, keepdims=True)
        p = jnp.exp(s - m)
        l = jnp.sum(p, axis=1, keepdims=True)
        pv = jnp.dot(p.astype(BF16), v, preferred_element_type=F32) * (1.0 / l)
        lse = m + jnp.log(l)
        o_acc = jnp.zeros((qb, DIL_GROUP_WIDTH), F32)
        lse_acc = jnp.zeros((qb, DIL_GROUP_WIDTH), F32)
        for h, hm in enumerate(head_masks):
            o_acc = jnp.where(hm, pv[h * qb:(h + 1) * qb], o_acc)
            lse_acc = jnp.where(hm, lse[h * qb:(h + 1) * qb], lse_acc)
        packed = _pack_halves(o_acc)
        if dil == 1:
            dst = pl.ds(a, qb)
        else:
            dst = pl.ds(a * dil + r, qb, stride=dil)
        o_ref[dst, :] = packed
        lse_lo_ref[dst, :] = lse_acc[:, :DIL_PACKED]
        lse_hi_ref[dst, :] = lse_acc[:, DIL_PACKED:]

    for r in range(dil):
        def body(i, carry, r=r):
            block(i, r)
            return carry
        lax.fori_loop(0, seq // qb, body, 0, unroll=min(DIL_UNROLL, seq // qb))


def _dilated(qa, ka, va, g):
    _, B, S, W = qa.shape
    dil = DIL_CONFIGS[g][1]
    in_spec = pl.BlockSpec((None, None, S, W), lambda b: (g, b, 0, 0))
    out_spec = pl.BlockSpec((None, S, W), lambda b: (b, 0, 0))
    return pl.pallas_call(
        functools.partial(_dil_kernel, dil=dil, seq=S // dil),
        grid=(B,),
        in_specs=[in_spec, in_spec, in_spec],
        out_specs=[out_spec, out_spec, out_spec],
        out_shape=[jax.ShapeDtypeStruct((B, S, W), U32), jax.ShapeDtypeStruct((B, S, W), F32),
                   jax.ShapeDtypeStruct((B, S, W), F32)],
        compiler_params=_params(("arbitrary",)),
        name=f"dilated{g}",
    )(qa, ka, va)


def _gqa_kernel(q_ref, kt_ref, v_ref, o_ref, s0_ref, s1_ref, p0_ref, p1_ref):
    tq = TQ_GQA
    s_refs, p_refs = (s0_ref, s1_ref), (p0_ref, p1_ref)
    lane = lax.broadcasted_iota(I32, (1, GQA_KV_WIDTH), 1)
    head_masks = [(lane >= g * HEAD_DIM) & (lane < (g + 1) * HEAD_DIM) for g in range(GQA_KV_HEADS)]
    nrep = GQA_ITEM_REPS
    items = [(sub, c0, g) for sub in range(q_ref.shape[0] // tq) for c0 in range(0, GQA_REP, nrep)
             for g in range(GQA_KV_HEADS)]

    def scores(i):
        sub, c0, g = items[i]
        rows = slice(sub * tq, (sub + 1) * tq)
        qg = jnp.concatenate(
            [jnp.where(head_masks[g], q_ref[rows, c * 256:(c + 1) * 256], jnp.zeros((tq, 256), BF16))
             for c in range(c0, c0 + nrep)], axis=0)
        s_refs[i % 2][...] = jnp.dot(qg, kt_ref[...], preferred_element_type=F32)

    scores(0)
    outs = None
    for i, (sub, c0, g) in enumerate(items):
        if i + 1 < len(items):
            scores(i + 1)
        if g == 0:
            outs = [jnp.zeros((tq, GQA_KV_WIDTH), F32) for _ in range(nrep)]
        s = s_refs[i % 2][...]
        m = jnp.max(s, axis=1, keepdims=True)
        e = jnp.exp(s - m)
        l = jnp.sum(e, axis=1, keepdims=True)
        p_refs[i % 2][...] = e.astype(BF16)
        pv = jnp.dot(p_refs[i % 2][...], v_ref[...], preferred_element_type=F32) * (1.0 / l)
        for j in range(nrep):
            outs[j] = jnp.where(head_masks[g], pv[j * tq:(j + 1) * tq], outs[j])
        if g == GQA_KV_HEADS - 1:
            rows = slice(sub * tq, (sub + 1) * tq)
            for j in range(nrep):
                o_ref[rows, (c0 + j) * 256:(c0 + j + 1) * 256] = outs[j].astype(BF16)


def _gqa(qb, kbt, vb):
    B, S, _ = qb.shape
    tq = TQ_GQA * GQA_SUBTILES
    item_rows = TQ_GQA * GQA_ITEM_REPS
    return pl.pallas_call(
        _gqa_kernel,
        grid=(B, S // tq),
        in_specs=[pl.BlockSpec((None, tq, GQA_Q_WIDTH), lambda b, i: (b, i, 0)),
                  pl.BlockSpec((None, GQA_KV_WIDTH, S), lambda b, i: (b, 0, 0)),
                  pl.BlockSpec((None, S, GQA_KV_WIDTH), lambda b, i: (b, 0, 0))],
        out_specs=pl.BlockSpec((None, tq, GQA_Q_WIDTH), lambda b, i: (b, i, 0)),
        out_shape=jax.ShapeDtypeStruct((B, S, GQA_Q_WIDTH), BF16),
        scratch_shapes=[pltpu.VMEM((item_rows, S), F32), pltpu.VMEM((item_rows, S), F32),
                        pltpu.VMEM((item_rows, S), BF16), pltpu.VMEM((item_rows, S), BF16)],
        compiler_params=_params(("arbitrary", "arbitrary")),
        name="gqa",
    )(qb, kbt, vb)


def _pack_halves(y):
    half = y.shape[1] // 2
    lo = lax.bitcast_convert_type(y[:, :half].astype(BF16).astype(F32), U32)
    hi = lax.bitcast_convert_type(y[:, half:].astype(BF16).astype(F32), U32)
    return (lo >> 16) | hi


def _unpack_halves(u):
    lo = lax.bitcast_convert_type(u << 16, F32)
    hi = lax.bitcast_convert_type(u & jnp.uint32(0xFFFF0000), F32)
    return lo, hi


ROW_SUBLANES = HALF // LANES


def _store_rows(ref, packed):
    m = packed.shape[0]
    for c in range(ROW_SUBLANES):
        ref[pl.ds(c, m, stride=ROW_SUBLANES), :] = packed[:, c * LANES:(c + 1) * LANES]


def _load_rows(ref, m):
    return jnp.concatenate([ref[pl.ds(c, m, stride=ROW_SUBLANES), :] for c in range(ROW_SUBLANES)], axis=1)


def _mix_kernel(x_ref, *refs):
    dil_refs, refs = refs[:3 * N_DIL_GROUPS], refs[3 * N_DIL_GROUPS:]
    (yb_ref, ga_ref, gb_ref, mod_ref, wod_ref, wog_ref, wout_ref, g2_ref, rw_ref, wsg_ref, wsu_ref, wsd_ref,
     xs_ref, h2_ref, logit_ref) = refs

    outs = [_unpack_halves(dil_refs[3 * g][0]) for g in range(N_DIL_GROUPS)]
    halves = []
    for part in range(2):
        lses = [dil_refs[3 * g + 1 + part][0] for g in range(N_DIL_GROUPS)]
        mx = functools.reduce(jnp.maximum, lses)
        es = [jnp.exp(l - mx) for l in lses]
        num = sum(e * outs[g][part] for g, e in enumerate(es))
        halves.append(num / sum(es))
    ya = jnp.concatenate(halves, axis=1)
    y_a = jnp.dot(ya.astype(BF16), wod_ref[...], preferred_element_type=F32)
    y_b = jnp.dot(yb_ref[0], wog_ref[...], preferred_element_type=F32)
    mix = ga_ref[0].astype(F32) * y_a + gb_ref[0].astype(F32) * y_b
    mixed = jnp.dot(mix.astype(BF16), wout_ref[...], preferred_element_type=F32)
    x1 = x_ref[0] + mod_ref[0, 2:3, :] * mixed

    ms = jnp.mean(x1 * x1, axis=-1, keepdims=True)
    h2 = x1 * lax.rsqrt(ms + NORM_EPS) * g2_ref[...]
    h2 = h2 * (1.0 + mod_ref[0, 4:5, :]) + mod_ref[0, 3:4, :]
    h2b = h2.astype(BF16)
    _store_rows(h2_ref, _pack_halves(h2))
    logit_ref[0] = jnp.dot(h2b, rw_ref[...], preferred_element_type=F32)

    gt = jnp.dot(h2b, wsg_ref[...], preferred_element_type=F32)
    up = jnp.dot(h2b, wsu_ref[...], preferred_element_type=F32)
    act = (gt * jax.nn.sigmoid(gt) * up).astype(BF16)
    shared = jnp.dot(act, wsd_ref[...], preferred_element_type=F32)
    xs_ref[0] = x1 + mod_ref[0, 5:6, :] * shared


def _mix(x, dil_outs, yb, ga, gb, mod, weights):
    B, S, _ = x.shape
    tm = TM_MIX
    tok = lambda w: pl.BlockSpec((1, tm, w), lambda b, i: (b, i, 0))
    dil_flat = [a for group in dil_outs for a in group]
    in_specs = ([tok(D_MODEL)] + [tok(DIL_PACKED)] * len(dil_flat) + [tok(GQA_Q_WIDTH), tok(D_MODEL), tok(D_MODEL),
                pl.BlockSpec((1, 6, D_MODEL), lambda b, i: (b, 0, 0))] + [_const_spec(w.shape) for w in weights])
    return pl.pallas_call(
        _mix_kernel,
        grid=(B, S // tm),
        in_specs=in_specs,
        out_specs=[tok(D_MODEL),
                   pl.BlockSpec((tm * ROW_SUBLANES, LANES), lambda b, i: (b * (S // tm) + i, 0)),
                   tok(N_EXPERTS)],
        out_shape=[jax.ShapeDtypeStruct((B, S, D_MODEL), F32),
                   jax.ShapeDtypeStruct((B * S * ROW_SUBLANES, LANES), U32),
                   jax.ShapeDtypeStruct((B, S, N_EXPERTS), F32)],
        compiler_params=_params(("arbitrary", "arbitrary")),
        name="mix",
    )(x, *dil_flat, yb, ga, gb, mod, *weights)


def _first_argmax(v, idx, big):
    m = jnp.max(v, axis=0, keepdims=True)
    i = jnp.min(jnp.where(v == m, idx, big), axis=0, keepdims=True)
    return m, i


def _route_kernel(logit_ref, bias_ref, tri_ref, idx_ref, gate_ref, gate_tm_ref, rank_ref, cnt_ref, base_ref):
    step = pl.program_id(0)
    tr = logit_ref.shape[0]

    @pl.when(step == 0)
    def _():
        base_ref[...] = jnp.zeros_like(base_ref)

    scores = jax.nn.sigmoid(logit_ref[...].T)
    biased = scores + bias_ref[...]
    row = lax.broadcasted_iota(I32, (N_EXPERTS, tr), 0)

    gscore = []
    group_row = lax.broadcasted_iota(I32, (GROUP_SIZE, tr), 0)
    for g in range(N_EXPERT_GROUPS):
        v, ri = biased[g * GROUP_SIZE:(g + 1) * GROUP_SIZE], group_row + g * GROUP_SIZE
        m1, i1 = _first_argmax(v, ri, N_EXPERTS)
        m2 = jnp.max(jnp.where(ri == i1, NEG_INF, v), axis=0, keepdims=True)
        gscore.append(m1 + m2)
    cur = jnp.concatenate(gscore, axis=0)
    gi = lax.broadcasted_iota(I32, (N_EXPERT_GROUPS, tr), 0)
    sel = jnp.zeros((N_EXPERT_GROUPS, tr), F32)
    for _ in range(TOPK_GROUPS):
        _, i = _first_argmax(cur, gi, N_EXPERT_GROUPS)
        hit = gi == i
        sel = jnp.where(hit, 1.0, sel)
        cur = jnp.where(hit, NEG_INF, cur)

    cur = jnp.concatenate(
        [jnp.where(sel[g:g + 1] > 0, biased[g * GROUP_SIZE:(g + 1) * GROUP_SIZE], NEG_INF)
         for g in range(N_EXPERT_GROUPS)], axis=0)
    idxs, gates, hits = [], [], []
    assigned = jnp.zeros((N_EXPERTS, tr), F32)
    for _ in range(TOP_K):
        _, i = _first_argmax(cur, row, N_EXPERTS)
        hit = row == i
        gates.append(jnp.sum(jnp.where(hit, scores, 0.0), axis=0, keepdims=True))
        cur = jnp.where(hit, NEG_INF, cur)
        assigned = jnp.where(hit, 1.0, assigned)
        idxs.append(i)
        hits.append(hit)
    gate = jnp.concatenate(gates, axis=0)
    gate = gate / jnp.sum(gate, axis=0, keepdims=True) * ROUTED_SCALE

    before = jnp.dot(assigned.astype(BF16), tri_ref[...], preferred_element_type=F32) + base_ref[:, 0:1]
    ranks = [jnp.sum(jnp.where(h, before, 0.0), axis=0, keepdims=True) for h in hits]

    idx_ref[...] = jnp.concatenate(idxs, axis=0)
    gate_ref[...] = gate
    rank_ref[...] = jnp.concatenate(ranks, axis=0).astype(I32)
    gate_tm_ref[...] = jnp.concatenate([gate, jnp.zeros((LANES - TOP_K, tr), F32)], axis=0).T
    base_ref[...] = base_ref[...] + jnp.sum(assigned, axis=1, keepdims=True)
    cnt_ref[...] = base_ref[...]


def _route(logits, bias):
    T = logits.shape[0]
    tr = TR_ROUTE
    tri = (lax.broadcasted_iota(I32, (tr, tr), 0) < lax.broadcasted_iota(I32, (tr, tr), 1)).astype(BF16)
    slot = pl.BlockSpec((TOP_K, tr), lambda i: (0, i))
    return pl.pallas_call(
        _route_kernel,
        grid=(T // tr,),
        in_specs=[pl.BlockSpec((tr, N_EXPERTS), lambda i: (i, 0)), _const_spec((N_EXPERTS, 1)), _const_spec((tr, tr))],
        out_specs=[slot, slot, pl.BlockSpec((tr, LANES), lambda i: (i, 0)), slot, _const_spec((N_EXPERTS, LANES))],
        out_shape=[jax.ShapeDtypeStruct((TOP_K, T), I32), jax.ShapeDtypeStruct((TOP_K, T), F32),
                   jax.ShapeDtypeStruct((T, LANES), F32), jax.ShapeDtypeStruct((TOP_K, T), I32),
                   jax.ShapeDtypeStruct((N_EXPERTS, LANES), F32)],
        scratch_shapes=[pltpu.VMEM((N_EXPERTS, LANES), F32)],
        compiler_params=_params(("arbitrary",)),
        name="route",
    )(logits, bias.reshape(N_EXPERTS, 1), tri)


def _plan_kernel(cnt_ref, idx_ref, rank_ref, ltri_ref, dest_ref, bexp_ref, nblk_ref, *, nb_max):
    tr = idx_ref.shape[1]
    blocks = jnp.floor((cnt_ref[...] + (BM_EXPERT - 1)) * (1.0 / BM_EXPERT))
    start = jnp.dot(ltri_ref[...], blocks.astype(BF16), preferred_element_type=F32)
    end = start + blocks
    blk = lax.broadcasted_iota(I32, (1, nb_max), 1).astype(F32)
    bexp = jnp.sum(jnp.where(end[:, 0:1] <= blk, 1.0, 0.0), axis=0, keepdims=True)
    bexp_ref[...] = jnp.minimum(bexp, N_EXPERTS - 1.0).astype(I32)
    nblk_ref[...] = jnp.sum(blocks, axis=0, keepdims=True).astype(I32)

    row = lax.broadcasted_iota(I32, (N_EXPERTS, tr), 0)
    start_rows = start[:, 0:1] * float(BM_EXPERT)
    dests = []
    for k in range(TOP_K):
        hit = row == idx_ref[k:k + 1, :]
        dests.append(jnp.sum(jnp.where(hit, start_rows, 0.0), axis=0, keepdims=True))
    dest = jnp.concatenate(dests, axis=0) + rank_ref[...].astype(F32)
    dest_ref[...] = jnp.concatenate([dest, jnp.zeros((LANES - TOP_K, tr), F32)], axis=0).T.astype(I32)


def _plan(counts, idx, rank, nb_max):
    T = idx.shape[1]
    tr = TR_ROUTE
    ltri = (lax.broadcasted_iota(I32, (N_EXPERTS, N_EXPERTS), 1)
            < lax.broadcasted_iota(I32, (N_EXPERTS, N_EXPERTS), 0)).astype(BF16)
    slot = pl.BlockSpec((TOP_K, tr), lambda i: (0, i))
    return pl.pallas_call(
        functools.partial(_plan_kernel, nb_max=nb_max),
        grid=(T // tr,),
        in_specs=[_const_spec((N_EXPERTS, LANES)), slot, slot, _const_spec((N_EXPERTS, N_EXPERTS))],
        out_specs=[pl.BlockSpec((tr, LANES), lambda i: (i, 0)), _const_spec((1, nb_max)), _const_spec((1, LANES))],
        out_shape=[jax.ShapeDtypeStruct((T, LANES), I32), jax.ShapeDtypeStruct((1, nb_max), I32),
                   jax.ShapeDtypeStruct((1, LANES), I32)],
        compiler_params=_params(("arbitrary",)),
        name="plan",
    )(counts, idx, rank, ltri)


def _dispatch_kernel(dest_ref, h_ref, xs_in_ref, xs_ref, sem):
    del xs_in_ref
    tm = h_ref.shape[0]

    def issue(t, carry):
        for k in range(TOP_K):
            pltpu.make_async_copy(h_ref.at[t], xs_ref.at[dest_ref[t * TOP_K + k]], sem).start(priority=k % 2)
        return carry

    lax.fori_loop(0, tm, issue, 0)
    for _ in range(TOP_K):
        pltpu.make_async_copy(h_ref, xs_ref.at[pl.ds(0, tm)], sem).wait()


def _dispatch(dest_flat, h_rows, xs0):
    T = h_rows.shape[0]
    rows_padded = xs0.shape[0]
    tm = TM_DISPATCH
    return pl.pallas_call(
        _dispatch_kernel,
        grid=(T // tm,),
        in_specs=[pl.BlockSpec((tm * TOP_K,), lambda i: (i,), memory_space=pltpu.SMEM),
                  pl.BlockSpec((tm, ROW_SUBLANES, LANES), lambda i: (i, 0, 0)),
                  pl.BlockSpec(memory_space=pl.ANY)],
        out_specs=pl.BlockSpec(memory_space=pl.ANY),
        out_shape=jax.ShapeDtypeStruct((rows_padded, ROW_SUBLANES, LANES), U32),
        scratch_shapes=[pltpu.SemaphoreType.DMA(())],
        input_output_aliases={2: 0},
        compiler_params=_params(("arbitrary",)),
        name="dispatch",
    )(dest_flat, h_rows, xs0)


def _expert_kernel(bexp_ref, nblk_ref, xs_ref, wg_ref, wu_ref, wd_ref, ys_ref, wgb_ref, wub_ref, wdb_ref,
                   xbuf_ref, xsem, obuf_ref, osem):
    p = pl.program_id(0)
    nb = nblk_ref[0]
    prev = bexp_ref[jnp.maximum(p - 1, 0)]
    active = p < nb
    block_rows = BM_EXPERT * ROW_SUBLANES

    def block(ref, q):
        return ref.at[pl.ds(pl.multiple_of(q * block_rows, block_rows), block_rows), :]

    def fetch(q):
        return pltpu.make_async_copy(block(xs_ref, q), xbuf_ref.at[q % XS_RING], xsem.at[q % XS_RING])

    def writeback(q):
        return pltpu.make_async_copy(obuf_ref.at[q % 2], block(ys_ref, q), osem.at[q % 2])

    @pl.when(p == 0)
    def _():
        fetch(0).start()

        @pl.when(nb > 1)
        def _():
            fetch(1).start()

    @pl.when(p + 2 < nb)
    def _():
        fetch(p + 2).start()

    @pl.when(active & ((p == 0) | (bexp_ref[p] != prev)))
    def _():
        wgb_ref[...] = wg_ref[...].astype(BF16)
        wub_ref[...] = wu_ref[...].astype(BF16)
        wdb_ref[...] = wd_ref[...].astype(BF16)

    @pl.when(active)
    def _():
        fetch(p).wait()
        lo, hi = _unpack_halves(_load_rows(xbuf_ref.at[p % XS_RING], BM_EXPERT))
        lo, hi = lo.astype(BF16), hi.astype(BF16)

        def proj(w_ref):
            return (jnp.dot(lo, w_ref[:HALF, :], preferred_element_type=F32)
                    + jnp.dot(hi, w_ref[HALF:, :], preferred_element_type=F32))

        gt, up = proj(wgb_ref), proj(wub_ref)
        act = (gt * jax.nn.sigmoid(gt) * up).astype(BF16)
        _store_rows(obuf_ref.at[p % 2], _pack_halves(jnp.dot(act, wdb_ref[...], preferred_element_type=F32)))
        writeback(p).start()

        @pl.when(p > 0)
        def _():
            writeback(p - 1).wait()

        @pl.when(p == nb - 1)
        def _():
            writeback(p).wait()


def _experts(bexp, nblk, xs, w_gate, w_up, w_down, nb_max):
    bm = BM_EXPERT
    wspec = lambda shape: pl.BlockSpec((None,) + shape, lambda p, be, nb: (be[p], 0, 0))
    block_shape = (bm * ROW_SUBLANES, LANES)
    grid_spec = pltpu.PrefetchScalarGridSpec(
        num_scalar_prefetch=2,
        grid=(nb_max,),
        in_specs=[pl.BlockSpec(memory_space=pl.ANY), wspec((D_MODEL, EXPERT_FF)),
                  wspec((D_MODEL, EXPERT_FF)), wspec((EXPERT_FF, D_MODEL))],
        out_specs=pl.BlockSpec(memory_space=pl.ANY),
        scratch_shapes=[pltpu.VMEM((D_MODEL, EXPERT_FF), BF16), pltpu.VMEM((D_MODEL, EXPERT_FF), BF16),
                        pltpu.VMEM((EXPERT_FF, D_MODEL), BF16),
                        pltpu.VMEM((XS_RING,) + block_shape, U32), pltpu.SemaphoreType.DMA((XS_RING,)),
                        pltpu.VMEM((2,) + block_shape, U32), pltpu.SemaphoreType.DMA((2,))],
    )
    return pl.pallas_call(
        _expert_kernel,
        grid_spec=grid_spec,
        out_shape=jax.ShapeDtypeStruct(xs.shape, U32),
        input_output_aliases={2: 0},
        compiler_params=_params(("arbitrary",)),
        name="experts",
    )(bexp, nblk, xs, w_gate, w_up, w_down)


def _combine_kernel(dest_ref, dnext_ref, xs_ref, gate_ref, mod_ref, fg_ref, ys_ref, ys2d_ref, o_ref, buf_ref, sem):
    i = pl.program_id(0)
    n = pl.num_programs(0)
    tm = o_ref.shape[0]
    slot = i % 2

    def row_copy(d_ref, s, t, k):
        return pltpu.make_async_copy(ys_ref.at[d_ref[t * TOP_K + k]],
                                     buf_ref.at[s, k, pl.ds(t * ROW_SUBLANES, ROW_SUBLANES), :], sem.at[s])

    @pl.when(i == 0)
    def _():
        def body(t, carry):
            for k in range(TOP_K):
                row_copy(dest_ref, 0, t, k).start(priority=k % 2)
            return carry
        lax.fori_loop(0, tm, body, 0)

    def finish(prefetch):
        for k in range(TOP_K):
            pltpu.make_async_copy(ys2d_ref.at[pl.ds(0, tm * ROW_SUBLANES), :], buf_ref.at[slot, k],
                                  sem.at[slot]).wait()
        acc_lo = jnp.zeros((tm, HALF), F32)
        acc_hi = jnp.zeros((tm, HALF), F32)
        batch = tm // TOP_K
        for k in range(TOP_K):
            if prefetch:
                for t in range(k * batch, (k + 1) * batch):
                    for kk in range(TOP_K):
                        row_copy(dnext_ref, 1 - slot, t, kk).start(priority=kk % 2)
            lo, hi = _unpack_halves(_load_rows(buf_ref.at[slot, k], tm))
            g = gate_ref[:, k:k + 1]
            acc_lo = acc_lo + g * lo
            acc_hi = acc_hi + g * hi
        routed = jnp.concatenate([acc_lo, acc_hi], axis=1)
        x2 = xs_ref[...] + mod_ref[0, 5:6, :] * routed
        ms = jnp.mean(x2 * x2, axis=-1, keepdims=True)
        o_ref[...] = x2 * lax.rsqrt(ms + NORM_EPS) * fg_ref[...]

    @pl.when(i + 1 < n)
    def _():
        finish(True)

    @pl.when(i + 1 >= n)
    def _():
        finish(False)


def _combine(dest_flat, xs_mid, gate_tm, mod, final_g, ys2d, seq):
    T = xs_mid.shape[0]
    tm = TM_COMBINE
    n = T // tm
    per_seq = seq // tm
    ys_rows = ys2d.reshape(-1, ROW_SUBLANES, LANES)
    return pl.pallas_call(
        _combine_kernel,
        grid=(n,),
        in_specs=[pl.BlockSpec((tm * TOP_K,), lambda i: (i,), memory_space=pltpu.SMEM),
                  pl.BlockSpec((tm * TOP_K,), lambda i: (jnp.minimum(i + 1, n - 1),), memory_space=pltpu.SMEM),
                  pl.BlockSpec((tm, D_MODEL), lambda i: (i, 0)),
                  pl.BlockSpec((tm, LANES), lambda i: (i, 0)),
                  pl.BlockSpec((1, 6, D_MODEL), lambda i: (i // per_seq, 0, 0)),
                  _const_spec((1, D_MODEL)),
                  pl.BlockSpec(memory_space=pl.ANY),
                  pl.BlockSpec(memory_space=pl.ANY)],
        out_specs=pl.BlockSpec((tm, D_MODEL), lambda i: (i, 0)),
        out_shape=jax.ShapeDtypeStruct((T, D_MODEL), F32),
        scratch_shapes=[pltpu.VMEM((2, TOP_K, tm * ROW_SUBLANES, LANES), U32), pltpu.SemaphoreType.DMA((2,))],
        compiler_params=_params(("arbitrary",)),
        name="combine",
    )(dest_flat, dest_flat, xs_mid, gate_tm, mod, final_g, ys_rows, ys2d)


def _rope_tables(seq):
    def cos_sin(pos, dim):
        inv = ROPE_THETA ** (-jnp.arange(0, dim, 2, dtype=F32) / dim)
        ang = pos.astype(F32)[:, None] * inv[None, :]
        ang = jnp.concatenate([ang, ang], axis=-1)
        sign = jnp.where(jnp.arange(dim) < dim // 2, -1.0, 1.0).astype(F32)
        return jnp.cos(ang), jnp.sin(ang) * sign

    t = jnp.arange(seq, dtype=I32)
    cos1, sin1 = cos_sin(t, HEAD_DIM)
    cos_r, sin_r = cos_sin(t // GRID_W, HEAD_DIM // 2)
    cos_c, sin_c = cos_sin(t % GRID_W, HEAD_DIM // 2)
    cosb = jnp.concatenate([cos_r, cos_c], axis=-1)
    sinb = jnp.concatenate([sin_r, sin_c], axis=-1)
    rep = LANES // HEAD_DIM
    return tuple(jnp.tile(a, (1, rep)) for a in (cos1, sin1, cosb, sinb))


def kernel(x, c, w_ada, b_ada, norm1_g, w_in, b_gate, qn_g, kn_g, w_o_dil, w_o_gqa, w_out, norm2_g,
           router_w, router_bias, w_exp_gate, w_exp_up, w_exp_down, w_sh_gate, w_sh_up, w_sh_down, final_g):
    B, S, D = x.shape
    T = B * S
    assert D == D_MODEL and S % (TM_PROJ * 1) == 0 and S // 16 >= QB_DIL
    row = lambda v: v.reshape(1, -1)

    mod = _adaln(c, w_ada, b_ada).reshape(B, 6, D)

    wb = w_in.astype(BF16)
    offs = [0, DIL_WIDTH, 2 * DIL_WIDTH, 3 * DIL_WIDTH, 3 * DIL_WIDTH + GQA_Q_WIDTH,
            3 * DIL_WIDTH + GQA_Q_WIDTH + GQA_KV_WIDTH, 3 * DIL_WIDTH + GQA_Q_WIDTH + 2 * GQA_KV_WIDTH]
    w_qa, w_ka, w_va = (wb[:, offs[i]:offs[i + 1]] for i in range(3))
    w_qb = wb[:, offs[3]:offs[4]].reshape(D, GQA_KV_HEADS, GQA_REP, HEAD_DIM).transpose(0, 2, 1, 3).reshape(D, GQA_Q_WIDTH)
    w_kb, w_vb = wb[:, offs[4]:offs[5]], wb[:, offs[5]:offs[6]]
    w_ga, w_gb = wb[:, offs[6]:offs[6] + D], wb[:, offs[6] + D:offs[6] + 2 * D]
    w_og = w_o_gqa.reshape(GQA_KV_HEADS, GQA_REP, HEAD_DIM, D).transpose(1, 0, 2, 3).reshape(GQA_Q_WIDTH, D).astype(BF16)

    lane_head = lax.broadcasted_iota(I32, (256, 256), 0) // HEAD_DIM
    bd = (lane_head == lane_head.T).astype(BF16)
    qn = jnp.tile(row(qn_g), (1, GQA_Q_WIDTH // HEAD_DIM))
    kn = jnp.tile(row(kn_g), (1, GQA_KV_WIDTH // HEAD_DIM))

    nb_max = -(-(T * TOP_K) // BM_EXPERT) + N_EXPERTS
    nb_max = -(-nb_max // LANES) * LANES
    rows_padded = nb_max * BM_EXPERT

    qa, ka, va, qb, kbt, vb, ga, gb, xs0 = _in_proj(
        x, mod, row(norm1_g), (w_qa, w_ka, w_va, w_qb, w_kb, w_vb, w_ga, w_gb), row(b_gate), qn, kn,
        _rope_tables(S), bd, rows_padded)

    dil_outs = [_dilated(qa, ka, va, g) for g in range(N_DIL_GROUPS)]
    yb = _gqa(qb, kbt, vb)

    xs_mid, h2p, logits = _mix(
        x, dil_outs, yb, ga, gb, mod,
        (w_o_dil.astype(BF16), w_og, w_out.astype(BF16), row(norm2_g), router_w.astype(BF16),
         w_sh_gate.astype(BF16), w_sh_up.astype(BF16), w_sh_down.astype(BF16)))

    idx, gate, gate_tm, rank, counts = _route(logits.reshape(T, N_EXPERTS), router_bias)
    del gate
    dest_tm, bexp, nblk = _plan(counts, idx, rank, nb_max)
    dest_flat = dest_tm[:, :TOP_K].reshape(T * TOP_K)

    xs = _dispatch(dest_flat, h2p.reshape(T, ROW_SUBLANES, LANES), xs0.reshape(rows_padded, ROW_SUBLANES, LANES))
    ys = _experts(bexp.reshape(nb_max), nblk.reshape(LANES)[:1], xs.reshape(rows_padded * ROW_SUBLANES, LANES),
                  w_exp_gate, w_exp_up, w_exp_down, nb_max)
    out = _combine(dest_flat, xs_mid.reshape(T, D), gate_tm, mod, row(final_g), ys, S)
    return out.reshape(B, S, D)
```
